```python
import jax, jax.numpy as jnp
from jax import lax
import numpy as np

D_MODEL = 1024
BATCH = 4
SEQ = 8192
DEPTH = 1
DEC_BATCH = 8
DEC_SEQ = 16
PAST_LEN = 4096

CHUNK = 64
EPS = 1e-6
LRU_WIDTH = D_MODEL // 2
LRU_BLOCKS = 8
LRU_BLOCK_W = LRU_WIDTH // LRU_BLOCKS
CONV_W = 4
LRU_C = 8.0
N_HEADS = 8
N_KV_HEADS = 2
HEAD_DIM = 64
ATTN_WIDTH = N_HEADS * HEAD_DIM
N_IDX_HEADS = 8
IDX_DIM = 64
TOPK_MAX = 256
Q_BLOCK = 128
ROPE_THETA = 10000.0
D_FF = ((8 * D_MODEL // 3 + 255) // 256) * 256
PROJ_SIZES = (LRU_WIDTH, LRU_WIDTH, N_HEADS * HEAD_DIM, N_KV_HEADS * HEAD_DIM,
              N_KV_HEADS * HEAD_DIM, N_IDX_HEADS * IDX_DIM, IDX_DIM, N_IDX_HEADS)
D_IN = sum(PROJ_SIZES)

kernel_name = "hymba_rglru_dsa_streaming_step"


def rmsnorm(x, g):
    xf = x.astype(jnp.float32)
    y = xf * lax.rsqrt(jnp.mean(xf * xf, axis=-1, keepdims=True) + EPS)
    return (y * g.astype(jnp.float32)).astype(x.dtype)


def rope(x, pos):
    half = x.shape[-1] // 2
    inv = ROPE_THETA ** (-jnp.arange(half, dtype=jnp.float32) / half)
    ang = pos.astype(jnp.float32)[:, None] * inv[None, :]
    cos = jnp.cos(ang)[None, :, None, :]
    sin = jnp.sin(ang)[None, :, None, :]
    xf = x.astype(jnp.float32)
    x1, x2 = xf[..., :half], xf[..., half:]
    return jnp.concatenate([x1 * cos - x2 * sin, x2 * cos + x1 * sin], axis=-1).astype(x.dtype)


def causal_conv(xr, prev, w, b):
    T = xr.shape[1]
    xpad = jnp.concatenate([prev.astype(xr.dtype), xr], axis=1)
    y = b
    for j in range(CONV_W):
        y = y + xpad[:, j:j + T] * w[j]
    return y, xpad[:, -(CONV_W - 1):]


def _lin_combine(c1, c2):
    a1, b1 = c1
    a2, b2 = c2
    return a1 * a2, a2 * b1 + b2


def rglru(x, pos, h_prev, w_rg, b_rg, w_ig, b_ig, lru_lambda):
    B, T, W = x.shape
    xf = x.astype(jnp.float32)
    xb = xf.reshape(B, T, LRU_BLOCKS, LRU_BLOCK_W)
    r = jax.nn.sigmoid(jnp.einsum('btni,nij->btnj', xb, w_rg.astype(jnp.float32)) + b_rg.astype(jnp.float32)).reshape(B, T, W)
    i = jax.nn.sigmoid(jnp.einsum('btni,nij->btnj', xb, w_ig.astype(jnp.float32)) + b_ig.astype(jnp.float32)).reshape(B, T, W)
    log_a = -LRU_C * r * jax.nn.softplus(-lru_lambda.astype(jnp.float32))
    a = jnp.exp(log_a)
    mult = jnp.where((pos == 0)[None, :, None], 1.0, jnp.sqrt(-jnp.expm1(2.0 * log_a)))
    bterm = mult * i * xf
    a_cum, b_cum = lax.associative_scan(_lin_combine, (a, bterm), axis=1)
    h = a_cum * h_prev.astype(jnp.float32)[:, None, :] + b_cum
    return h, h[:, -1]


def dsa_block(q, qi, wi, q_pos, k, v, ki, k_pos, topk):
    B, Tq = q.shape[:2]
    sc = jnp.einsum('bthd,bsd->bths', qi.astype(jnp.float32), ki.astype(jnp.float32))
    idx_score = jnp.einsum('bths,bth->bts', jax.nn.relu(sc), wi.astype(jnp.float32))
    adm = (q_pos[:, None] // CHUNK) >= (k_pos[None, :] // CHUNK)
    idx_score = jnp.where(adm[None], idx_score, -jnp.inf)
    vals, sel = lax.top_k(idx_score, topk)
    valid = jnp.isfinite(vals)
    gather = jax.vmap(lambda arr, ix: arr[ix])
    ks = gather(k, sel)
    vs = gather(v, sel)
    qg = q.reshape(B, Tq, N_KV_HEADS, N_HEADS // N_KV_HEADS, HEAD_DIM)
    s = jnp.einsum('bthgd,btkhd->bthgk', qg.astype(jnp.float32), ks.astype(jnp.float32)) * (HEAD_DIM ** -0.5)
    s = jnp.where(valid[:, :, None, None, :], s, -jnp.inf)
    p = jax.nn.softmax(s, axis=-1)
    o = jnp.einsum('bthgk,btkhd->bthgd', p, vs.astype(jnp.float32))
    return o.reshape(B, Tq, N_HEADS * HEAD_DIM).astype(q.dtype)


def dsa_attend(q, qi, wi, q_pos, k, v, ki, k_pos, topk):
    B, T = q.shape[:2]
    if T > Q_BLOCK and T % Q_BLOCK == 0:
        nb = T // Q_BLOCK

        def split_blocks(arr):
            return jnp.moveaxis(arr.reshape((B, nb, Q_BLOCK) + arr.shape[2:]), 1, 0)

        def one(args):
            qb, qib, wib, pb = args
            return dsa_block(qb, qib, wib, pb, k, v, ki, k_pos, topk)

        out = lax.map(one, (split_blocks(q), split_blocks(qi), split_blocks(wi), q_pos.reshape(nb, Q_BLOCK)))
        return jnp.moveaxis(out, 0, 1).reshape(B, T, N_HEADS * HEAD_DIM)
    return dsa_block(q, qi, wi, q_pos, k, v, ki, k_pos, topk)


def layer(x, pos, conv_prev, h_prev, k_past, v_past, ki_past,
          norm_mix, w_in, conv_w, conv_b, w_rg, b_rg, w_ig, b_ig, lru_lambda,
          q_norm, k_norm, w_out, norm_ffn, w_ffn_in, w_ffn_out):
    B, T, _ = x.shape
    h = rmsnorm(x, norm_mix)
    proj = h @ w_in
    parts = []
    off = 0
    for size in PROJ_SIZES:
        parts.append(proj[..., off:off + size])
        off += size
    xr, gate, q, k, v, qi, ki, wi = parts
    xc, conv_new = causal_conv(xr, conv_prev, conv_w, conv_b)
    hs, h_last = rglru(xc, pos, h_prev, w_rg, b_rg, w_ig, b_ig, lru_lambda)
    y_a = hs.astype(x.dtype) * jax.nn.gelu(gate)
    q = rope(rmsnorm(q.reshape(B, T, N_HEADS, HEAD_DIM), q_norm), pos)
    k = rope(rmsnorm(k.reshape(B, T, N_KV_HEADS, HEAD_DIM), k_norm), pos)
    v = v.reshape(B, T, N_KV_HEADS, HEAD_DIM)
    qi = rope(qi.reshape(B, T, N_IDX_HEADS, IDX_DIM), pos)
    ki = rope(ki.reshape(B, T, 1, IDX_DIM), pos)[:, :, 0]
    wi = wi * (N_IDX_HEADS ** -0.5 * IDX_DIM ** -0.5)
    if k_past is None:
        k_all, v_all, ki_all, k_pos = k, v, ki, pos
    else:
        P = k_past.shape[1]
        k_all = jnp.concatenate([k_past.astype(k.dtype), k], axis=1)
        v_all = jnp.concatenate([v_past.astype(v.dtype), v], axis=1)
        ki_all = jnp.concatenate([ki_past.astype(ki.dtype), ki], axis=1)
        k_pos = jnp.concatenate([jnp.arange(P, dtype=jnp.int32), pos])
    topk = min(TOPK_MAX, k_all.shape[1] // 4)
    y_b = dsa_attend(q, qi, wi, pos, k_all, v_all, ki_all, k_pos, topk)
    x = x + jnp.concatenate([y_a, y_b], axis=-1) @ w_out
    hf = rmsnorm(x, norm_ffn)
    gu = hf @ w_ffn_in
    x = x + (jax.nn.silu(gu[..., :D_FF]) * gu[..., D_FF:]) @ w_ffn_out
    return x, k, v, ki, h_last.astype(x.dtype), conv_new


def setup_inputs(seed: int = 0) -> dict:
    key = jax.random.key(seed)
    ks = jax.random.split(key, 24)
    f32 = jnp.float32
    nrm = lambda kk, shape, s: jax.random.normal(kk, shape, f32) * s
    a8 = jax.random.uniform(ks[10], (LRU_WIDTH,), f32, 0.9, 0.999)
    a_base = a8 ** (1.0 / LRU_C)
    return {
        "x_prompt": nrm(ks[0], (BATCH, SEQ, D_MODEL), 1.0),
        "x_sample": nrm(ks[1], (DEC_BATCH, DEC_SEQ, D_MODEL), 1.0),
        "cache_k": nrm(ks[2], (DEC_BATCH, PAST_LEN, N_KV_HEADS, HEAD_DIM), 1.0),
        "cache_v": nrm(ks[3], (DEC_BATCH, PAST_LEN, N_KV_HEADS, HEAD_DIM), 1.0),
        "cache_kidx": nrm(ks[4], (DEC_BATCH, PAST_LEN, IDX_DIM), 1.0),
        "state_h": nrm(ks[5], (DEC_BATCH, LRU_WIDTH), 0.5),
        "state_conv": nrm(ks[6], (DEC_BATCH, CONV_W - 1, LRU_WIDTH), 1.0),
        "norm_mix": 1.0 + nrm(ks[7], (D_MODEL,), 0.01),
        "w_in": nrm(ks[8], (D_MODEL, D_IN), D_MODEL ** -0.5),
        "conv_w": nrm(ks[9], (CONV_W, LRU_WIDTH), CONV_W ** -0.5),
        "conv_b": nrm(ks[11], (LRU_WIDTH,), 0.01),
        "w_rg": nrm(ks[12], (LRU_BLOCKS, LRU_BLOCK_W, LRU_BLOCK_W), LRU_BLOCK_W ** -0.5),
        "b_rg": nrm(ks[13], (LRU_BLOCKS, LRU_BLOCK_W), 0.01),
        "w_ig": nrm(ks[14], (LRU_BLOCKS, LRU_BLOCK_W, LRU_BLOCK_W), LRU_BLOCK_W ** -0.5),
        "b_ig": nrm(ks[15], (LRU_BLOCKS, LRU_BLOCK_W), 0.01),
        "lru_lambda": jnp.log(a_base) - jnp.log1p(-a_base),
        "q_norm": 1.0 + nrm(ks[16], (HEAD_DIM,), 0.01),
        "k_norm": 1.0 + nrm(ks[17], (HEAD_DIM,), 0.01),
        "w_out": nrm(ks[18], (LRU_WIDTH + ATTN_WIDTH, D_MODEL), (LRU_WIDTH + ATTN_WIDTH) ** -0.5),
        "norm_ffn": 1.0 + nrm(ks[19], (D_MODEL,), 0.01),
        "w_ffn_in": nrm(ks[20], (D_MODEL, 2 * D_FF), D_MODEL ** -0.5),
        "w_ffn_out": nrm(ks[21], (D_FF, D_MODEL), D_FF ** -0.5),
    }


def reference(x_prompt, x_sample, cache_k, cache_v, cache_kidx, state_h, state_conv,
              norm_mix, w_in, conv_w, conv_b, w_rg, b_rg, w_ig, b_ig, lru_lambda,
              q_norm, k_norm, w_out, norm_ffn, w_ffn_in, w_ffn_out):
    weights = (norm_mix, w_in, conv_w, conv_b, w_rg, b_rg, w_ig, b_ig, lru_lambda,
               q_norm, k_norm, w_out, norm_ffn, w_ffn_in, w_ffn_out)
    Bp, Tp, _ = x_prompt.shape
    Bs, Ts, _ = x_sample.shape
    P = cache_k.shape[1]
    pos_p = jnp.arange(Tp, dtype=jnp.int32)
    pos_s = P + jnp.arange(Ts, dtype=jnp.int32)
    yp = x_prompt
    conv0 = jnp.zeros((Bp, CONV_W - 1, LRU_WIDTH), x_prompt.dtype)
    h0 = jnp.zeros((Bp, LRU_WIDTH), x_prompt.dtype)
    ys = x_sample
    for _ in range(DEPTH):
        yp, k_p, v_p, ki_p, h_p, conv_p = layer(yp, pos_p, conv0, h0, None, None, None, *weights)
        ys, k_s, v_s, ki_s, h_s, conv_s = layer(ys, pos_s, state_conv, state_h, cache_k, cache_v, cache_kidx, *weights)
    return (yp, ys, k_p, v_p, ki_p, h_p, conv_p, k_s, v_s, ki_s, h_s, conv_s)
```

```python
import functools

import numpy as np
import jax
import jax.numpy as jnp
from jax import lax
from jax.experimental import pallas as pl
from jax.experimental.pallas import tpu as pltpu

F32 = jnp.float32
BF16 = jnp.bfloat16

CHUNK = 64
EPS = 1e-6
LRU_BLOCKS = 8
CONV_W = 4
LRU_C = 8.0
N_HEADS = 8
N_KV_HEADS = 2
HEAD_DIM = 64
N_IDX_HEADS = 8
IDX_DIM = 64
TOPK_MAX = 256
ROPE_THETA = 10000.0

LANES = 128
SUBLANES = 8
VMEM_LIMIT = 48 * 1024 * 1024
INT_MIN = -2 ** 31
NEG_BIG = -1e30
BF16_SUBLANES = 16
V_ROWS = HEAD_DIM + BF16_SUBLANES


def _cparams(sem):
    return pltpu.CompilerParams(dimension_semantics=sem, vmem_limit_bytes=VMEM_LIMIT)


def _const_spec(shape):
    nd = len(shape)
    return pl.BlockSpec(shape, lambda *_: (0,) * nd)


def _swap_halves(z):
    lane = lax.broadcasted_iota(jnp.int32, z.shape, 1)
    lo_half = (lane % HEAD_DIM) < (HEAD_DIM // 2)
    return jnp.where(lo_half, pltpu.roll(z, LANES - HEAD_DIM // 2, 1), pltpu.roll(z, HEAD_DIM // 2, 1))


def _rope(z, cos, sin):
    outs = []
    for s in range(z.shape[1] // LANES):
        zs = z[:, s * LANES:(s + 1) * LANES]
        outs.append(zs * cos + _swap_halves(zs) * sin)
    return outs[0] if len(outs) == 1 else jnp.concatenate(outs, axis=1)


def _head_rmsnorm(z, gain, g):
    z2 = z * z
    hi = z2.astype(BF16)
    lo = (z2 - hi.astype(F32)).astype(BF16)
    ss = jnp.dot(hi, g, preferred_element_type=F32) + jnp.dot(lo, g, preferred_element_type=F32)
    return z * lax.rsqrt(ss * (1.0 / HEAD_DIM) + EPS) * gain


def _gelu_tanh(x):
    return 0.5 * x * (1.0 + jnp.tanh(np.sqrt(2.0 / np.pi).astype(np.float32) * (x + 0.044715 * (x * x * x))))


def _proj_kernel(x_ref, cos_ref, sin_ref, cprev_ref, hprev_ref,
                 nmix_ref, wa_ref, wq_ref, wkv_ref, wqi_ref, wkw_ref,
                 convw_ref, convb_ref, wrg_ref, wig_ref, brg_ref, big_ref, lam_ref,
                 qn_ref, kn_ref, g_ref,
                 ya_ref, q_ref, qi_ref, kv_ref, kw_ref, tail_ref, hlast_ref,
                 cbuf, hcar, *, tb, pos0, wi_scale):
    t = pl.program_id(1)
    w = cbuf.shape[1]

    @pl.when(t == 0)
    def _():
        cbuf[0:SUBLANES, :] = cprev_ref[0]
        hcar[...] = hprev_ref[0]

    x = x_ref[0]
    ms = jnp.mean(x * x, axis=-1, keepdims=True)
    h = (x * lax.rsqrt(ms + EPS) * nmix_ref[...]).astype(BF16)

    pa = jnp.dot(h, wa_ref[...], preferred_element_type=F32)
    xr = pa[:, :w]
    gate = pa[:, w:]
    cbuf[SUBLANES:SUBLANES + tb, :] = xr
    xc = convb_ref[...]
    for j in range(CONV_W):
        off = SUBLANES - (CONV_W - 1) + j
        xc = xc + cbuf[off:off + tb, :] * convw_ref[j:j + 1, :]
    tail = cbuf[tb:tb + SUBLANES, :]
    cbuf[0:SUBLANES, :] = tail
    tail_ref[0] = tail

    xcb = xc.astype(BF16)
    r = jax.nn.sigmoid(jnp.dot(xcb, wrg_ref[...], preferred_element_type=F32) + brg_ref[...])
    ig = jax.nn.sigmoid(jnp.dot(xcb, wig_ref[...], preferred_element_type=F32) + big_ref[...])
    nl = -lam_ref[...]
    softplus = jnp.maximum(nl, 0.0) + jnp.log(1.0 + jnp.exp(-jnp.abs(nl)))
    log_a = (-LRU_C) * r * softplus
    a = jnp.exp(log_a)
    mult = jnp.sqrt(1.0 - jnp.exp(2.0 * log_a))
    row = lax.broadcasted_iota(jnp.int32, (tb, 1), 0)
    mult = jnp.where(row + (pos0 + t * tb) == 0, 1.0, mult)
    b = mult * ig * xc
    s = 1
    while s < tb:
        keep = row >= s
        a_s = jnp.where(keep, pltpu.roll(a, s, 0), 1.0)
        b_s = jnp.where(keep, pltpu.roll(b, s, 0), 0.0)
        b = a * b_s + b
        a = a * a_s
        s *= 2
    hs = a * hcar[...] + b
    hcar[...] = hs[tb - 1:tb, :]
    hlast_ref[0] = hs[tb - SUBLANES:tb, :]
    ya_ref[0] = (hs * _gelu_tanh(gate)).astype(ya_ref.dtype)

    cos = cos_ref[...]
    sin = sin_ref[...]
    g = g_ref[...]
    pq = jnp.dot(h, wq_ref[...], preferred_element_type=F32)
    q = _rope(_head_rmsnorm(pq, qn_ref[...], g), cos, sin)
    q_ref[0] = (q * (HEAD_DIM ** -0.5)).astype(q_ref.dtype)

    pkv = jnp.dot(h, wkv_ref[...], preferred_element_type=F32)
    kw_ = N_KV_HEADS * HEAD_DIM
    k = _rope(_head_rmsnorm(pkv[:, :kw_], kn_ref[...], g[:kw_, :kw_]), cos, sin)
    kv_ref[0] = jnp.concatenate([k, pkv[:, kw_:]], axis=1)

    pqi = jnp.dot(h, wqi_ref[...], preferred_element_type=F32)
    qi_ref[0] = _rope(pqi, cos, sin).astype(qi_ref.dtype)

    pkw = jnp.dot(h, wkw_ref[...], preferred_element_type=F32)
    lane = lax.broadcasted_iota(jnp.int32, pkw.shape, 1)
    kw_ref[0] = jnp.where(lane < IDX_DIM, _rope(pkw, cos, sin), pkw * wi_scale)


def _proj_call(x, cos, sin, cprev, hprev, wts, *, tb, pos0):
    bsz, t_len, d = x.shape
    w = wts["conv_b"].shape[1]
    aw = N_HEADS * HEAD_DIM
    kvw = 2 * N_KV_HEADS * HEAD_DIM
    nt = t_len // tb
    wi_scale = float(N_IDX_HEADS ** -0.5 * IDX_DIM ** -0.5)
    names = ("norm_mix", "w_a", "w_q", "w_kv", "w_qi", "w_kw", "conv_w", "conv_b",
             "w_rg", "w_ig", "b_rg", "b_ig", "lam", "q_norm", "k_norm", "g")
    warrs = [wts[n] for n in names]
    in_specs = [
        pl.BlockSpec((1, tb, d), lambda b, t: (b, t, 0)),
        pl.BlockSpec((tb, LANES), lambda b, t: (t, 0)),
        pl.BlockSpec((tb, LANES), lambda b, t: (t, 0)),
        pl.BlockSpec((1, SUBLANES, w), lambda b, t: (b, 0, 0)),
        pl.BlockSpec((1, 1, w), lambda b, t: (b, 0, 0)),
    ] + [_const_spec(a.shape) for a in warrs]
    out_shape = (
        jax.ShapeDtypeStruct((bsz, t_len, w), BF16),
        jax.ShapeDtypeStruct((bsz, t_len, aw), BF16),
        jax.ShapeDtypeStruct((bsz, t_len, aw), BF16),
        jax.ShapeDtypeStruct((bsz, t_len, kvw), F32),
        jax.ShapeDtypeStruct((bsz, t_len, LANES), F32),
        jax.ShapeDtypeStruct((bsz, SUBLANES, w), F32),
        jax.ShapeDtypeStruct((bsz, SUBLANES, w), F32),
    )
    out_specs = (
        pl.BlockSpec((1, tb, w), lambda b, t: (b, t, 0)),
        pl.BlockSpec((1, tb, aw), lambda b, t: (b, t, 0)),
        pl.BlockSpec((1, tb, aw), lambda b, t: (b, t, 0)),
        pl.BlockSpec((1, tb, kvw), lambda b, t: (b, t, 0)),
        pl.BlockSpec((1, tb, LANES), lambda b, t: (b, t, 0)),
        pl.BlockSpec((1, SUBLANES, w), lambda b, t: (b, 0, 0)),
        pl.BlockSpec((1, SUBLANES, w), lambda b, t: (b, 0, 0)),
    )
    return pl.pallas_call(
        functools.partial(_proj_kernel, tb=tb, pos0=pos0, wi_scale=wi_scale),
        grid=(bsz, nt),
        in_specs=in_specs,
        out_specs=out_specs,
        out_shape=out_shape,
        scratch_shapes=[pltpu.VMEM((SUBLANES + tb, w), F32), pltpu.VMEM((1, w), F32)],
        compiler_params=_cparams(("arbitrary", "arbitrary")),
        name="proj",
    )(x, cos, sin, cprev, hprev, *warrs)


def _dsa_kernel(qT_ref, qiT_ref, wiT_ref, k_ref, vT_ref, ki_ref, o_ref,
                keys_scr, m_scr, acc_scr, *, tq, sk, s_real, n_kblocks, q_pos0, n_q_real, topk):
    i = pl.program_id(1)
    q_idx = i * tq + lax.broadcasted_iota(jnp.int32, (1, tq), 1)
    q_adm_end = ((q_pos0 + q_idx) // CHUNK + 1) * CHUNK
    q_adm_end = jnp.minimum(q_adm_end, s_real)
    last_q = jnp.minimum(i * tq + tq - 1, n_q_real - 1)
    blk_end = jnp.minimum(((q_pos0 + last_q) // CHUNK + 1) * CHUNK, s_real)
    nkb = jnp.minimum((blk_end + sk - 1) // sk, n_kblocks)

    def score_body(j, carry):
        off = pl.multiple_of(j * sk, sk)
        kib = ki_ref[0, pl.ds(off, sk), :]
        acc = jnp.zeros((sk, tq), F32)
        for hh in range(N_IDX_HEADS):
            sc = jnp.dot(kib, qiT_ref[0, hh * IDX_DIM:(hh + 1) * IDX_DIM, :], preferred_element_type=F32)
            acc = acc + jnp.maximum(sc, 0.0) * wiT_ref[0, hh:hh + 1, :]
        bits = lax.bitcast_convert_type(acc, jnp.int32)
        key = jnp.where(bits < 0, INT_MIN - bits, bits)
        kidx = off + lax.broadcasted_iota(jnp.int32, (sk, 1), 0)
        key = jnp.where(kidx < q_adm_end, key, INT_MIN)
        keys_scr[pl.ds(off, sk), :] = key
        return carry

    lax.fori_loop(0, nkb, score_body, 0)

    def count_ge(thr):
        def body(j, cnt):
            off = pl.multiple_of(j * sk, sk)
            ge = (keys_scr[pl.ds(off, sk), :] >= thr).astype(jnp.int32)
            return cnt + jnp.sum(ge.reshape(sk // SUBLANES, SUBLANES, tq), axis=0)
        cnt = lax.fori_loop(0, nkb, body, jnp.zeros((SUBLANES, tq), jnp.int32))
        return jnp.sum(cnt, axis=0, keepdims=True)

    open0 = jnp.logical_and(q_adm_end > topk, q_idx < n_q_real).astype(jnp.int32)
    tu0 = jnp.zeros((1, tq), jnp.int32)

    def search_cond(st):
        bit, _, _, n_open = st
        return jnp.logical_and(bit >= 0, n_open > 0)

    def search_body(st):
        bit, tu, opn, _ = st
        cand = tu | lax.shift_left(jnp.int32(1), bit)
        cnt = count_ge(cand ^ INT_MIN)
        accept = jnp.logical_and(cnt >= topk, opn > 0)
        tu = jnp.where(accept, cand, tu)
        opn = jnp.where(jnp.logical_and(accept, cnt == topk), 0, opn)
        return bit - 1, tu, opn, jnp.sum(opn)

    _, tu, opn, n_open = lax.while_loop(search_cond, search_body, (jnp.int32(31), tu0, open0, jnp.sum(open0)))
    thr = jnp.maximum(tu ^ INT_MIN, INT_MIN + 1)

    @pl.when(n_open > 0)
    def _():
        cnt_gt = count_ge(thr + 1)
        need = jnp.where(opn > 0, topk - cnt_gt, jnp.int32(2 ** 30)).astype(F32)
        r_i = lax.broadcasted_iota(jnp.int32, (sk, sk), 0)
        c_i = lax.broadcasted_iota(jnp.int32, (sk, sk), 1)
        tri = (c_i <= r_i).astype(BF16)

        def tie_body(j, seen):
            off = pl.multiple_of(j * sk, sk)
            key = keys_scr[pl.ds(off, sk), :]
            eq = key == thr
            rank = jnp.dot(tri, eq.astype(BF16), preferred_element_type=F32) + seen
            keys_scr[pl.ds(off, sk), :] = jnp.where(jnp.logical_and(eq, rank > need), INT_MIN, key)
            return rank[sk - 1:sk, :]

        lax.fori_loop(0, nkb, tie_body, jnp.zeros((1, tq), F32))

    m_scr[...] = jnp.full(m_scr.shape, NEG_BIG, F32)
    acc_scr[...] = jnp.zeros(acc_scr.shape, F32)
    grp = N_HEADS // N_KV_HEADS

    def attn_body(j, carry):
        off = pl.multiple_of(j * sk, sk)
        sel = keys_scr[pl.ds(off, sk), :] >= thr
        for gk in range(N_KV_HEADS):
            kb = k_ref[0, pl.ds(off, sk), gk * HEAD_DIM:(gk + 1) * HEAD_DIM]
            vtb = vT_ref[0, j, gk * V_ROWS:(gk + 1) * V_ROWS, :]
            for hq in range(grp):
                hh = gk * grp + hq
                s = jnp.dot(kb, qT_ref[0, hh * HEAD_DIM:(hh + 1) * HEAD_DIM, :], preferred_element_type=F32)
                s = jnp.where(sel, s, NEG_BIG)
                m_old = m_scr[hh]
                m_new = jnp.maximum(m_old, jnp.max(s, axis=0, keepdims=True))
                p = jnp.exp(s - m_new)
                alpha = jnp.exp(m_old - m_new)
                acc_scr[hh] = alpha * acc_scr[hh] + jnp.dot(vtb, p.astype(BF16), preferred_element_type=F32)
                m_scr[hh] = m_new
        return carry

    lax.fori_loop(0, nkb, attn_body, 0)

    outs = []
    for hh in range(N_HEADS):
        acc = acc_scr[hh]
        outs.append(acc[:HEAD_DIM, :] / acc[HEAD_DIM:HEAD_DIM + 1, :])
    o_ref[0] = jnp.concatenate(outs, axis=0).T.astype(o_ref.dtype)


def _dsa_call(qT, qiT, wiT, k_b, vT_blk, ki_b, *, tq, sk, s_real, q_pos0, n_q_real, topk):
    bsz, aw, tq_total = qT.shape
    n_kblocks = vT_blk.shape[1]
    s_pad = k_b.shape[1]
    nq = tq_total // tq
    kern = functools.partial(_dsa_kernel, tq=tq, sk=sk, s_real=s_real, n_kblocks=n_kblocks,
                             q_pos0=q_pos0, n_q_real=n_q_real, topk=topk)
    return pl.pallas_call(
        kern,
        grid=(bsz, nq),
        in_specs=[
            pl.BlockSpec((1, aw, tq), lambda b, i: (b, 0, i)),
            pl.BlockSpec((1, aw, tq), lambda b, i: (b, 0, i)),
            pl.BlockSpec((1, N_IDX_HEADS, tq), lambda b, i: (b, 0, i)),
            pl.BlockSpec((1, s_pad, k_b.shape[2]), lambda b, i: (b, 0, 0)),
            pl.BlockSpec((1, n_kblocks, N_KV_HEADS * V_ROWS, sk), lambda b, i: (b, 0, 0, 0)),
            pl.BlockSpec((1, s_pad, IDX_DIM), lambda b, i: (b, 0, 0)),
        ],
        out_specs=pl.BlockSpec((1, tq, aw), lambda b, i: (b, i, 0)),
        out_shape=jax.ShapeDtypeStruct((bsz, tq_total, aw), BF16),
        scratch_shapes=[
            pltpu.VMEM((s_pad, tq), jnp.int32),
            pltpu.VMEM((N_HEADS, 1, tq), F32),
            pltpu.VMEM((N_HEADS, V_ROWS, tq), F32),
        ],
        compiler_params=_cparams(("arbitrary", "arbitrary")),
        name="dsa",
    )(qT, qiT, wiT, k_b, vT_blk, ki_b)


def _ffn_kernel(x_ref, ya_ref, yb_ref, woa_ref, wob_ref, nf_ref, wg_ref, wu_ref, wo_ref, o_ref, *, n_chunks):
    x1 = x_ref[...] + jnp.dot(ya_ref[...], woa_ref[...], preferred_element_type=F32)
    x1 = x1 + jnp.dot(yb_ref[...], wob_ref[...], preferred_element_type=F32)
    ms = jnp.mean(x1 * x1, axis=-1, keepdims=True)
    hf = (x1 * lax.rsqrt(ms + EPS) * nf_ref[...]).astype(BF16)

    def body(c, acc):
        gt = jnp.dot(hf, wg_ref[c], preferred_element_type=F32)
        up = jnp.dot(hf, wu_ref[c], preferred_element_type=F32)
        act = (gt * jax.nn.sigmoid(gt) * up).astype(BF16)
        return acc + jnp.dot(act, wo_ref[c], preferred_element_type=F32)

    o_ref[...] = lax.fori_loop(0, n_chunks, body, x1)


def _ffn_call(x2, ya2, yb2, wts, *, tm):
    n, d = x2.shape
    w = ya2.shape[1]
    aw = yb2.shape[1]
    n_chunks, _, fc = wts["w_g"].shape
    single = pl.Buffered(1)

    def wspec(shape):
        nd = len(shape)
        return pl.BlockSpec(shape, lambda *_: (0,) * nd, pipeline_mode=single)

    return pl.pallas_call(
        functools.partial(_ffn_kernel, n_chunks=n_chunks),
        grid=(n // tm,),
        in_specs=[
            pl.BlockSpec((tm, d), lambda r: (r, 0)),
            pl.BlockSpec((tm, w), lambda r: (r, 0)),
            pl.BlockSpec((tm, aw), lambda r: (r, 0)),
            wspec(wts["w_out_a"].shape), wspec(wts["w_out_b"].shape), wspec(wts["norm_ffn"].shape),
            wspec(wts["w_g"].shape), wspec(wts["w_u"].shape), wspec(wts["w_o"].shape),
        ],
        out_specs=pl.BlockSpec((tm, d), lambda r: (r, 0)),
        out_shape=jax.ShapeDtypeStruct((n, d), F32),
        compiler_params=_cparams(("arbitrary",)),
        name="ffn",
    )(x2, ya2, yb2, wts["w_out_a"], wts["w_out_b"], wts["norm_ffn"], wts["w_g"], wts["w_u"], wts["w_o"])


def _block_diag(wb):
    n, a, b = wb.shape
    eye = jnp.eye(n, dtype=wb.dtype)
    return (eye[:, None, :, None] * wb[:, :, None, :]).reshape(n * a, n * b)


def _ffn_chunk(d_ff):
    for c in (512, 256, 128):
        if d_ff % c == 0:
            return c
    return d_ff


def _prep_weights(norm_mix, w_in, conv_w, conv_b, w_rg, b_rg, w_ig, b_ig, lru_lambda,
                  q_norm, k_norm, w_out, norm_ffn, w_ffn_in, w_ffn_out):
    d = w_in.shape[0]
    w = conv_b.shape[0]
    aw = N_HEADS * HEAD_DIM
    kw = N_KV_HEADS * HEAD_DIM
    iw = N_IDX_HEADS * IDX_DIM
    o = np.cumsum([0, w, w, aw, kw, kw, iw, IDX_DIM, N_IDX_HEADS])
    wb = w_in.astype(BF16)
    w_kw = jnp.pad(wb[:, o[6]:o[8]], ((0, 0), (0, LANES - (IDX_DIM + N_IDX_HEADS))))
    d_ff = w_ffn_out.shape[0]
    fc = _ffn_chunk(d_ff)
    nck = d_ff // fc
    wfi = w_ffn_in.astype(BF16)
    hd_id = np.arange(aw) // HEAD_DIM
    return {
        "norm_mix": norm_mix.reshape(1, d),
        "w_a": wb[:, o[0]:o[2]],
        "w_q": wb[:, o[2]:o[3]],
        "w_kv": wb[:, o[3]:o[5]],
        "w_qi": wb[:, o[5]:o[6]],
        "w_kw": w_kw,
        "conv_w": jnp.pad(conv_w, ((0, SUBLANES - CONV_W), (0, 0))),
        "conv_b": conv_b.reshape(1, w),
        "w_rg": _block_diag(w_rg).astype(BF16),
        "w_ig": _block_diag(w_ig).astype(BF16),
        "b_rg": b_rg.reshape(1, w),
        "b_ig": b_ig.reshape(1, w),
        "lam": lru_lambda.reshape(1, w),
        "q_norm": jnp.tile(q_norm, N_HEADS).reshape(1, aw),
        "k_norm": jnp.tile(k_norm, N_KV_HEADS).reshape(1, kw),
        "g": jnp.asarray(hd_id[:, None] == hd_id[None, :], BF16),
        "w_out_a": w_out[:w].astype(BF16),
        "w_out_b": w_out[w:].astype(BF16),
        "norm_ffn": norm_ffn.reshape(1, d),
        "w_g": wfi[:, :d_ff].reshape(d, nck, fc).transpose(1, 0, 2),
        "w_u": wfi[:, d_ff:].reshape(d, nck, fc).transpose(1, 0, 2),
        "w_o": w_ffn_out.astype(BF16).reshape(nck, fc, d),
    }


def _rope_tables(pos):
    half = HEAD_DIM // 2
    inv = ROPE_THETA ** (-jnp.arange(half, dtype=F32) / half)
    ang = pos.astype(F32)[:, None] * inv[None, :]
    cos = jnp.cos(ang)
    sin = jnp.sin(ang)
    reps = LANES // HEAD_DIM
    return (jnp.tile(jnp.concatenate([cos, cos], axis=1), (1, reps)),
            jnp.tile(jnp.concatenate([-sin, sin], axis=1), (1, reps)))


def _round_up(n, m):
    return (n + m - 1) // m * m


def _layer(x, pos0, conv_prev, h_prev, k_past, v_past, ki_past, wts):
    bsz, t_len, d = x.shape
    w = conv_prev.shape[2]
    kw = N_KV_HEADS * HEAD_DIM
    pos = pos0 + jnp.arange(t_len, dtype=jnp.int32)
    cos, sin = _rope_tables(pos)
    tb = 256 if t_len % 256 == 0 else t_len
    cprev = jnp.pad(conv_prev, ((0, 0), (SUBLANES - (CONV_W - 1), 0), (0, 0)))
    ya, q, qi, kv, kwi, tail, hlast = _proj_call(
        x, cos, sin, cprev, h_prev.reshape(bsz, 1, w), wts, tb=tb, pos0=pos0)
    k_new = kv[..., :kw]
    v_new = kv[..., kw:]
    ki_new = kwi[..., :IDX_DIM]
    wi = kwi[..., IDX_DIM:IDX_DIM + N_IDX_HEADS]

    if k_past is None:
        k_all, v_all, ki_all = k_new, v_new, ki_new
    else:
        p_len = k_past.shape[1]
        k_all = jnp.concatenate([k_past.reshape(bsz, p_len, kw), k_new], axis=1)
        v_all = jnp.concatenate([v_past.reshape(bsz, p_len, kw), v_new], axis=1)
        ki_all = jnp.concatenate([ki_past, ki_new], axis=1)
    s_real = k_all.shape[1]
    topk = min(TOPK_MAX, s_real // 4)

    tq = 256 if t_len % 256 == 0 else LANES
    tq_total = _round_up(t_len, tq)
    sk = 256
    s_pad = _round_up(s_real, sk)
    nkb = s_pad // sk
    pad_q = ((0, 0), (0, tq_total - t_len), (0, 0))
    pad_s = ((0, 0), (0, s_pad - s_real), (0, 0))
    qT = jnp.pad(q, pad_q).transpose(0, 2, 1)
    qiT = jnp.pad(qi, pad_q).transpose(0, 2, 1)
    wiT = jnp.pad(wi, pad_q).transpose(0, 2, 1)
    k_b = jnp.pad(k_all.astype(BF16), pad_s)
    ki_b = jnp.pad(ki_all.astype(BF16), pad_s)
    v_b = jnp.pad(v_all.astype(BF16), pad_s).reshape(bsz, nkb, sk, N_KV_HEADS, HEAD_DIM)
    vT = v_b.transpose(0, 1, 3, 4, 2)
    ones = jnp.ones((bsz, nkb, N_KV_HEADS, 1, sk), BF16)
    zeros = jnp.zeros((bsz, nkb, N_KV_HEADS, V_ROWS - HEAD_DIM - 1, sk), BF16)
    vT_blk = jnp.concatenate([vT, ones, zeros], axis=3).reshape(bsz, nkb, N_KV_HEADS * V_ROWS, sk)
    yb = _dsa_call(qT, qiT, wiT, k_b, vT_blk, ki_b, tq=tq, sk=sk, s_real=s_real,
                   q_pos0=pos0, n_q_real=t_len, topk=topk)[:, :t_len]

    n = bsz * t_len
    tm = 512 if n % 512 == 0 else n
    y = _ffn_call(x.reshape(n, d), ya.reshape(n, w), yb.reshape(n, N_HEADS * HEAD_DIM), wts, tm=tm)
    return (y.reshape(bsz, t_len, d),
            k_new.reshape(bsz, t_len, N_KV_HEADS, HEAD_DIM),
            v_new.reshape(bsz, t_len, N_KV_HEADS, HEAD_DIM),
            ki_new,
            hlast[:, SUBLANES - 1],
            tail[:, SUBLANES - (CONV_W - 1):])


def kernel(x_prompt, x_sample, cache_k, cache_v, cache_kidx, state_h, state_conv, norm_mix, w_in, conv_w, conv_b, w_rg, b_rg, w_ig, b_ig, lru_lambda, q_norm, k_norm, w_out, norm_ffn, w_ffn_in, w_ffn_out):
    wts = _prep_weights(norm_mix, w_in, conv_w, conv_b, w_rg, b_rg, w_ig, b_ig, lru_lambda,
                        q_norm, k_norm, w_out, norm_ffn, w_ffn_in, w_ffn_out)
    bp = x_prompt.shape[0]
    w = conv_b.shape[0]
    p_len = cache_k.shape[1]
    conv0 = jnp.zeros((bp, CONV_W - 1, w), x_prompt.dtype)
    h0 = jnp.zeros((bp, w), x_prompt.dtype)
    yp, k_p, v_p, ki_p, h_p, conv_p = _layer(x_prompt, 0, conv0, h0, None, None, None, wts)
    ys, k_s, v_s, ki_s, h_s, conv_s = _layer(x_sample, p_len, state_conv, state_h,
                                             cache_k, cache_v, cache_kidx, wts)
    return (yp, ys, k_p, v_p, ki_p, h_p, conv_p, k_s, v_s, ki_s, h_s, conv_s)
```

```python
import functools

import numpy as np
import jax
import jax.numpy as jnp
from jax import lax
from jax.experimental import pallas as pl
from jax.experimental.pallas import tpu as pltpu

F32 = jnp.float32
BF16 = jnp.bfloat16

CHUNK = 64
EPS = 1e-6
LRU_BLOCKS = 8
CONV_W = 4
LRU_C = 8.0
N_HEADS = 8
N_KV_HEADS = 2
HEAD_DIM = 64
N_IDX_HEADS = 8
IDX_DIM = 64
TOPK_MAX = 256
ROPE_THETA = 10000.0

LANES = 128
SUBLANES = 8
VMEM_LIMIT = 48 * 1024 * 1024
INT_MIN = -2 ** 31
NEG_BIG = -1e30
LOG2_E = float(np.log2(np.e))
BF16_SUBLANES = 16
V_ROWS = HEAD_DIM + BF16_SUBLANES


def _cparams(sem):
    return pltpu.CompilerParams(dimension_semantics=sem, vmem_limit_bytes=VMEM_LIMIT)


def _const_spec(shape):
    nd = len(shape)
    return pl.BlockSpec(shape, lambda *_: (0,) * nd)


def _swap_halves(z):
    lane = lax.broadcasted_iota(jnp.int32, z.shape, 1)
    lo_half = (lane % HEAD_DIM) < (HEAD_DIM // 2)
    return jnp.where(lo_half, pltpu.roll(z, LANES - HEAD_DIM // 2, 1), pltpu.roll(z, HEAD_DIM // 2, 1))


def _rope(z, cos, sin):
    outs = []
    for s in range(z.shape[1] // LANES):
        zs = z[:, s * LANES:(s + 1) * LANES]
        outs.append(zs * cos + _swap_halves(zs) * sin)
    return outs[0] if len(outs) == 1 else jnp.concatenate(outs, axis=1)


def _head_rmsnorm(z, gain, g):
    z2 = z * z
    hi = z2.astype(BF16)
    lo = (z2 - hi.astype(F32)).astype(BF16)
    ss = jnp.dot(hi, g, preferred_element_type=F32) + jnp.dot(lo, g, preferred_element_type=F32)
    return z * lax.rsqrt(ss * (1.0 / HEAD_DIM) + EPS) * gain


def _gelu_tanh(x):
    return 0.5 * x * (1.0 + jnp.tanh(np.sqrt(2.0 / np.pi).astype(np.float32) * (x + 0.044715 * (x * x * x))))


def _proj_kernel(x_ref, cos_ref, sin_ref, cprev_ref, hprev_ref,
                 nmix_ref, wa_ref, wq_ref, wkv_ref, wqi_ref, wkw_ref,
                 convw_ref, convb_ref, wrg_ref, wig_ref, brg_ref, big_ref, lam_ref,
                 qn_ref, kn_ref, g_ref,
                 ya_ref, q_ref, qi_ref, kv_ref, kw_ref, tail_ref, hlast_ref,
                 cbuf, hcar, *, tb, pos0, wi_scale):
    t = pl.program_id(1)
    w = cbuf.shape[1]

    @pl.when(t == 0)
    def _():
        cbuf[0:SUBLANES, :] = cprev_ref[0]
        hcar[...] = hprev_ref[0]

    x = x_ref[0]
    ms = jnp.mean(x * x, axis=-1, keepdims=True)
    h = (x * lax.rsqrt(ms + EPS) * nmix_ref[...]).astype(BF16)

    pa = jnp.dot(h, wa_ref[...], preferred_element_type=F32)
    xr = pa[:, :w]
    gate = pa[:, w:]
    cbuf[SUBLANES:SUBLANES + tb, :] = xr
    xc = convb_ref[...]
    for j in range(CONV_W):
        off = SUBLANES - (CONV_W - 1) + j
        xc = xc + cbuf[off:off + tb, :] * convw_ref[j:j + 1, :]
    tail = cbuf[tb:tb + SUBLANES, :]
    cbuf[0:SUBLANES, :] = tail
    tail_ref[0] = tail

    xcb = xc.astype(BF16)
    r = jax.nn.sigmoid(jnp.dot(xcb, wrg_ref[...], preferred_element_type=F32) + brg_ref[...])
    ig = jax.nn.sigmoid(jnp.dot(xcb, wig_ref[...], preferred_element_type=F32) + big_ref[...])
    nl = -lam_ref[...]
    softplus = jnp.maximum(nl, 0.0) + jnp.log(1.0 + jnp.exp(-jnp.abs(nl)))
    log_a = (-LRU_C) * r * softplus
    a = jnp.exp(log_a)
    mult = jnp.sqrt(1.0 - jnp.exp(2.0 * log_a))
    row = lax.broadcasted_iota(jnp.int32, (tb, 1), 0)
    mult = jnp.where(row + (pos0 + t * tb) == 0, 1.0, mult)
    b = mult * ig * xc
    s = 1
    while s < tb:
        keep = row >= s
        a_s = jnp.where(keep, pltpu.roll(a, s, 0), 1.0)
        b_s = jnp.where(keep, pltpu.roll(b, s, 0), 0.0)
        b = a * b_s + b
        a = a * a_s
        s *= 2
    hs = a * hcar[...] + b
    hcar[...] = hs[tb - 1:tb, :]
    hlast_ref[0] = hs[tb - SUBLANES:tb, :]
    ya_ref[0] = (hs * _gelu_tanh(gate)).astype(ya_ref.dtype)

    cos = cos_ref[...]
    sin = sin_ref[...]
    g = g_ref[...]
    pq = jnp.dot(h, wq_ref[...], preferred_element_type=F32)
    q = _rope(_head_rmsnorm(pq, qn_ref[...], g), cos, sin)
    q_ref[0] = (q * (LOG2_E * HEAD_DIM ** -0.5)).astype(q_ref.dtype)

    pkv = jnp.dot(h, wkv_ref[...], preferred_element_type=F32)
    kw_ = N_KV_HEADS * HEAD_DIM
    k = _rope(_head_rmsnorm(pkv[:, :kw_], kn_ref[...], g[:kw_, :kw_]), cos, sin)
    kv_ref[0] = jnp.concatenate([k, pkv[:, kw_:]], axis=1)

    pqi = jnp.dot(h, wqi_ref[...], preferred_element_type=F32)
    qi_ref[0] = _rope(pqi, cos, sin).astype(qi_ref.dtype)

    pkw = jnp.dot(h, wkw_ref[...], preferred_element_type=F32)
    lane = lax.broadcasted_iota(jnp.int32, pkw.shape, 1)
    kw_ref[0] = jnp.where(lane < IDX_DIM, _rope(pkw, cos, sin), pkw * wi_scale)


def _proj_call(x, cos, sin, cprev, hprev, wts, *, tb, pos0):
    bsz, t_len, d = x.shape
    w = wts["conv_b"].shape[1]
    aw = N_HEADS * HEAD_DIM
    kvw = 2 * N_KV_HEADS * HEAD_DIM
    nt = t_len // tb
    wi_scale = float(N_IDX_HEADS ** -0.5 * IDX_DIM ** -0.5)
    names = ("norm_mix", "w_a", "w_q", "w_kv", "w_qi", "w_kw", "conv_w", "conv_b",
             "w_rg", "w_ig", "b_rg", "b_ig", "lam", "q_norm", "k_norm", "g")
    warrs = [wts[n] for n in names]
    in_specs = [
        pl.BlockSpec((1, tb, d), lambda b, t: (b, t, 0)),
        pl.BlockSpec((tb, LANES), lambda b, t: (t, 0)),
        pl.BlockSpec((tb, LANES), lambda b, t: (t, 0)),
        pl.BlockSpec((1, SUBLANES, w), lambda b, t: (b, 0, 0)),
        pl.BlockSpec((1, 1, w), lambda b, t: (b, 0, 0)),
    ] + [_const_spec(a.shape) for a in warrs]
    out_shape = (
        jax.ShapeDtypeStruct((bsz, t_len, w), BF16),
        jax.ShapeDtypeStruct((bsz, t_len, aw), BF16),
        jax.ShapeDtypeStruct((bsz, t_len, aw), BF16),
        jax.ShapeDtypeStruct((bsz, t_len, kvw), F32),
        jax.ShapeDtypeStruct((bsz, t_len, LANES), F32),
        jax.ShapeDtypeStruct((bsz, SUBLANES, w), F32),
        jax.ShapeDtypeStruct((bsz, SUBLANES, w), F32),
    )
    out_specs = (
        pl.BlockSpec((1, tb, w), lambda b, t: (b, t, 0)),
        pl.BlockSpec((1, tb, aw), lambda b, t: (b, t, 0)),
        pl.BlockSpec((1, tb, aw), lambda b, t: (b, t, 0)),
        pl.BlockSpec((1, tb, kvw), lambda b, t: (b, t, 0)),
        pl.BlockSpec((1, tb, LANES), lambda b, t: (b, t, 0)),
        pl.BlockSpec((1, SUBLANES, w), lambda b, t: (b, 0, 0)),
        pl.BlockSpec((1, SUBLANES, w), lambda b, t: (b, 0, 0)),
    )
    return pl.pallas_call(
        functools.partial(_proj_kernel, tb=tb, pos0=pos0, wi_scale=wi_scale),
        grid=(bsz, nt),
        in_specs=in_specs,
        out_specs=out_specs,
        out_shape=out_shape,
        scratch_shapes=[pltpu.VMEM((SUBLANES + tb, w), F32), pltpu.VMEM((1, w), F32)],
        compiler_params=_cparams(("arbitrary", "arbitrary")),
        name="proj",
    )(x, cos, sin, cprev, hprev, *warrs)


def _dsa_kernel(qT_ref, qiT_ref, wiT_ref, k_ref, vT_ref, ki_ref, o_ref,
                keys_scr, m_scr, acc_scr, *, tq, sk, s_real, n_kblocks, q_pos0, n_q_real, topk):
    i = pl.program_id(1)
    q_idx = i * tq + lax.broadcasted_iota(jnp.int32, (1, tq), 1)
    q_adm_end = ((q_pos0 + q_idx) // CHUNK + 1) * CHUNK
    q_adm_end = jnp.minimum(q_adm_end, s_real)
    last_q = jnp.minimum(i * tq + tq - 1, n_q_real - 1)
    blk_end = jnp.minimum(((q_pos0 + last_q) // CHUNK + 1) * CHUNK, s_real)
    nkb = jnp.minimum((blk_end + sk - 1) // sk, n_kblocks)

    def score_body(j, carry):
        off = pl.multiple_of(j * sk, sk)
        kib = ki_ref[0, pl.ds(off, sk), :]
        acc = jnp.zeros((sk, tq), F32)
        for hh in range(N_IDX_HEADS):
            sc = jnp.dot(kib, qiT_ref[0, hh * IDX_DIM:(hh + 1) * IDX_DIM, :], preferred_element_type=F32)
            acc = acc + jnp.maximum(sc, 0.0) * wiT_ref[0, hh:hh + 1, :]
        bits = lax.bitcast_convert_type(acc, jnp.int32)
        key = jnp.where(bits < 0, INT_MIN - bits, bits)
        kidx = off + lax.broadcasted_iota(jnp.int32, (sk, 1), 0)
        key = jnp.where(kidx < q_adm_end, key, INT_MIN)
        keys_scr[pl.ds(off, sk), :] = key
        return carry

    lax.fori_loop(0, nkb, score_body, 0)

    def count_ge(thr):
        def body(j, cnt):
            off = pl.multiple_of(j * sk, sk)
            ge = (keys_scr[pl.ds(off, sk), :] >= thr).astype(jnp.int32)
            return cnt + jnp.sum(ge.reshape(sk // SUBLANES, SUBLANES, tq), axis=0)
        cnt = lax.fori_loop(0, nkb, body, jnp.zeros((SUBLANES, tq), jnp.int32))
        return jnp.sum(cnt, axis=0, keepdims=True)

    open0 = jnp.logical_and(q_adm_end > topk, q_idx < n_q_real).astype(jnp.int32)
    tu0 = jnp.zeros((1, tq), jnp.int32)

    def search_cond(st):
        bit, _, _, n_open = st
        return jnp.logical_and(bit >= 0, n_open > 0)

    def search_body(st):
        bit, tu, opn, _ = st
        cand = tu | lax.shift_left(jnp.int32(1), bit)
        cnt = count_ge(cand ^ INT_MIN)
        accept = jnp.logical_and(cnt >= topk, opn > 0)
        tu = jnp.where(accept, cand, tu)
        opn = jnp.where(jnp.logical_and(accept, cnt == topk), 0, opn)
        return bit - 1, tu, opn, jnp.sum(opn)

    _, tu, opn, n_open = lax.while_loop(search_cond, search_body, (jnp.int32(31), tu0, open0, jnp.sum(open0)))
    thr = jnp.maximum(tu ^ INT_MIN, INT_MIN + 1)

    @pl.when(n_open > 0)
    def _():
        cnt_gt = count_ge(thr + 1)
        need = jnp.where(opn > 0, topk - cnt_gt, jnp.int32(2 ** 30)).astype(F32)
        r_i = lax.broadcasted_iota(jnp.int32, (sk, sk), 0)
        c_i = lax.broadcasted_iota(jnp.int32, (sk, sk), 1)
        tri = (c_i <= r_i).astype(BF16)

        def tie_body(j, seen):
            off = pl.multiple_of(j * sk, sk)
            key = keys_scr[pl.ds(off, sk), :]
            eq = key == thr
            rank = jnp.dot(tri, eq.astype(BF16), preferred_element_type=F32) + seen
            keys_scr[pl.ds(off, sk), :] = jnp.where(jnp.logical_and(eq, rank > need), INT_MIN, key)
            return rank[sk - 1:sk, :]

        lax.fori_loop(0, nkb, tie_body, jnp.zeros((1, tq), F32))

    m_scr[...] = jnp.full(m_scr.shape, NEG_BIG, F32)
    acc_scr[...] = jnp.zeros(acc_scr.shape, F32)
    grp = N_HEADS // N_KV_HEADS

    def attn_body(j, carry):
        off = pl.multiple_of(j * sk, sk)
        sel = keys_scr[pl.ds(off, sk), :] >= thr
        logits = []
        for gk in range(N_KV_HEADS):
            kb = k_ref[0, pl.ds(off, sk), gk * HEAD_DIM:(gk + 1) * HEAD_DIM]
            logits.append(jnp.dot(kb, qT_ref[0, 0, gk * HEAD_DIM:(gk + 1) * HEAD_DIM, :],
                                  preferred_element_type=F32))
        for gk in range(N_KV_HEADS):
            vtb = vT_ref[0, j, gk * V_ROWS:(gk + 1) * V_ROWS, :]
            s = logits[gk]
            s = jnp.concatenate(
                [jnp.where(sel, s[:, hq * tq:(hq + 1) * tq], NEG_BIG) for hq in range(grp)], axis=1)
            m_old = m_scr[gk]
            m_new = jnp.maximum(m_old, jnp.max(s, axis=0, keepdims=True))
            p = jnp.exp2(s - m_new).astype(BF16)
            alpha = jnp.exp2(m_old - m_new)
            acc_scr[gk] = alpha * acc_scr[gk] + jnp.dot(vtb, p, preferred_element_type=F32)
            m_scr[gk] = m_new
        return carry

    lax.fori_loop(0, nkb, attn_body, 0)

    outs = []
    for gk in range(N_KV_HEADS):
        acc = acc_scr[gk]
        for hq in range(grp):
            sl = slice(hq * tq, (hq + 1) * tq)
            outs.append(acc[:HEAD_DIM, sl] / acc[HEAD_DIM:HEAD_DIM + 1, sl])
    o_ref[0] = jnp.concatenate(outs, axis=0).T.astype(o_ref.dtype)


def _dsa_call(qg, qiT, wiT, k_b, vT_blk, ki_b, *, tq, sk, s_real, q_pos0, n_q_real, topk):
    bsz, aw, tq_total = qiT.shape
    n_kblocks = vT_blk.shape[1]
    s_pad = k_b.shape[1]
    nq = tq_total // tq
    grp = N_HEADS // N_KV_HEADS
    kern = functools.partial(_dsa_kernel, tq=tq, sk=sk, s_real=s_real, n_kblocks=n_kblocks,
                             q_pos0=q_pos0, n_q_real=n_q_real, topk=topk)
    return pl.pallas_call(
        kern,
        grid=(bsz, nq),
        in_specs=[
            pl.BlockSpec((1, 1, N_KV_HEADS * HEAD_DIM, grp * tq), lambda b, i: (b, i, 0, 0)),
            pl.BlockSpec((1, aw, tq), lambda b, i: (b, 0, i)),
            pl.BlockSpec((1, N_IDX_HEADS, tq), lambda b, i: (b, 0, i)),
            pl.BlockSpec((1, s_pad, k_b.shape[2]), lambda b, i: (b, 0, 0)),
            pl.BlockSpec((1, n_kblocks, N_KV_HEADS * V_ROWS, sk), lambda b, i: (b, 0, 0, 0)),
            pl.BlockSpec((1, s_pad, IDX_DIM), lambda b, i: (b, 0, 0)),
        ],
        out_specs=pl.BlockSpec((1, tq, aw), lambda b, i: (b, i, 0)),
        out_shape=jax.ShapeDtypeStruct((bsz, tq_total, aw), BF16),
        scratch_shapes=[
            pltpu.VMEM((s_pad, tq), jnp.int32),
            pltpu.VMEM((N_KV_HEADS, 1, grp * tq), F32),
            pltpu.VMEM((N_KV_HEADS, V_ROWS, grp * tq), F32),
        ],
        compiler_params=_cparams(("arbitrary", "arbitrary")),
        name="dsa",
    )(qg, qiT, wiT, k_b, vT_blk, ki_b)


def _ffn_kernel(x_ref, ya_ref, yb_ref, woa_ref, wob_ref, nf_ref, wg_ref, wu_ref, wo_ref, o_ref, *, n_chunks):
    x1 = x_ref[...] + jnp.dot(ya_ref[...], woa_ref[...], preferred_element_type=F32)
    x1 = x1 + jnp.dot(yb_ref[...], wob_ref[...], preferred_element_type=F32)
    ms = jnp.mean(x1 * x1, axis=-1, keepdims=True)
    hf = (x1 * lax.rsqrt(ms + EPS) * nf_ref[...]).astype(BF16)

    def body(c, acc):
        gt = jnp.dot(hf, wg_ref[c], preferred_element_type=F32)
        up = jnp.dot(hf, wu_ref[c], preferred_element_type=F32)
        act = (gt * jax.nn.sigmoid(gt) * up).astype(BF16)
        return acc + jnp.dot(act, wo_ref[c], preferred_element_type=F32)

    o_ref[...] = lax.fori_loop(0, n_chunks, body, x1)


def _ffn_call(x2, ya2, yb2, wts, *, tm):
    n, d = x2.shape
    w = ya2.shape[1]
    aw = yb2.shape[1]
    n_chunks, _, fc = wts["w_g"].shape
    single = pl.Buffered(1)

    def wspec(shape):
        nd = len(shape)
        return pl.BlockSpec(shape, lambda *_: (0,) * nd, pipeline_mode=single)

    return pl.pallas_call(
        functools.partial(_ffn_kernel, n_chunks=n_chunks),
        grid=(n // tm,),
        in_specs=[
            pl.BlockSpec((tm, d), lambda r: (r, 0)),
            pl.BlockSpec((tm, w), lambda r: (r, 0)),
            pl.BlockSpec((tm, aw), lambda r: (r, 0)),
            wspec(wts["w_out_a"].shape), wspec(wts["w_out_b"].shape), wspec(wts["norm_ffn"].shape),
            wspec(wts["w_g"].shape), wspec(wts["w_u"].shape), wspec(wts["w_o"].shape),
        ],
        out_specs=pl.BlockSpec((tm, d), lambda r: (r, 0)),
        out_shape=jax.ShapeDtypeStruct((n, d), F32),
        compiler_params=_cparams(("arbitrary",)),
        name="ffn",
    )(x2, ya2, yb2, wts["w_out_a"], wts["w_out_b"], wts["norm_ffn"], wts["w_g"], wts["w_u"], wts["w_o"])


def _block_diag(wb):
    n, a, b = wb.shape
    eye = jnp.eye(n, dtype=wb.dtype)
    return (eye[:, None, :, None] * wb[:, :, None, :]).reshape(n * a, n * b)


def _ffn_chunk(d_ff):
    for c in (512, 256, 128):
        if d_ff % c == 0:
            return c
    return d_ff


def _prep_weights(norm_mix, w_in, conv_w, conv_b, w_rg, b_rg, w_ig, b_ig, lru_lambda,
                  q_norm, k_norm, w_out, norm_ffn, w_ffn_in, w_ffn_out):
    d = w_in.shape[0]
    w = conv_b.shape[0]
    aw = N_HEADS * HEAD_DIM
    kw = N_KV_HEADS * HEAD_DIM
    iw = N_IDX_HEADS * IDX_DIM
    o = np.cumsum([0, w, w, aw, kw, kw, iw, IDX_DIM, N_IDX_HEADS])
    wb = w_in.astype(BF16)
    w_kw = jnp.pad(wb[:, o[6]:o[8]], ((0, 0), (0, LANES - (IDX_DIM + N_IDX_HEADS))))
    d_ff = w_ffn_out.shape[0]
    fc = _ffn_chunk(d_ff)
    nck = d_ff // fc
    wfi = w_ffn_in.astype(BF16)
    hd_id = np.arange(aw) // HEAD_DIM
    return {
        "norm_mix": norm_mix.reshape(1, d),
        "w_a": wb[:, o[0]:o[2]],
        "w_q": wb[:, o[2]:o[3]],
        "w_kv": wb[:, o[3]:o[5]],
        "w_qi": wb[:, o[5]:o[6]],
        "w_kw": w_kw,
        "conv_w": jnp.pad(conv_w, ((0, SUBLANES - CONV_W), (0, 0))),
        "conv_b": conv_b.reshape(1, w),
        "w_rg": _block_diag(w_rg).astype(BF16),
        "w_ig": _block_diag(w_ig).astype(BF16),
        "b_rg": b_rg.reshape(1, w),
        "b_ig": b_ig.reshape(1, w),
        "lam": lru_lambda.reshape(1, w),
        "q_norm": jnp.tile(q_norm, N_HEADS).reshape(1, aw),
        "k_norm": jnp.tile(k_norm, N_KV_HEADS).reshape(1, kw),
        "g": jnp.asarray(hd_id[:, None] == hd_id[None, :], BF16),
        "w_out_a": w_out[:w].astype(BF16),
        "w_out_b": w_out[w:].astype(BF16),
        "norm_ffn": norm_ffn.reshape(1, d),
        "w_g": wfi[:, :d_ff].reshape(d, nck, fc).transpose(1, 0, 2),
        "w_u": wfi[:, d_ff:].reshape(d, nck, fc).transpose(1, 0, 2),
        "w_o": w_ffn_out.astype(BF16).reshape(nck, fc, d),
    }


def _rope_tables(pos):
    half = HEAD_DIM // 2
    inv = ROPE_THETA ** (-jnp.arange(half, dtype=F32) / half)
    ang = pos.astype(F32)[:, None] * inv[None, :]
    cos = jnp.cos(ang)
    sin = jnp.sin(ang)
    reps = LANES // HEAD_DIM
    return (jnp.tile(jnp.concatenate([cos, cos], axis=1), (1, reps)),
            jnp.tile(jnp.concatenate([-sin, sin], axis=1), (1, reps)))


def _round_up(n, m):
    return (n + m - 1) // m * m


def _layer(x, pos0, conv_prev, h_prev, k_past, v_past, ki_past, wts):
    bsz, t_len, d = x.shape
    w = conv_prev.shape[2]
    kw = N_KV_HEADS * HEAD_DIM
    pos = pos0 + jnp.arange(t_len, dtype=jnp.int32)
    cos, sin = _rope_tables(pos)
    tb = 256 if t_len % 256 == 0 else t_len
    cprev = jnp.pad(conv_prev, ((0, 0), (SUBLANES - (CONV_W - 1), 0), (0, 0)))
    ya, q, qi, kv, kwi, tail, hlast = _proj_call(
        x, cos, sin, cprev, h_prev.reshape(bsz, 1, w), wts, tb=tb, pos0=pos0)
    k_new = kv[..., :kw]
    v_new = kv[..., kw:]
    ki_new = kwi[..., :IDX_DIM]
    wi = kwi[..., IDX_DIM:IDX_DIM + N_IDX_HEADS]

    if k_past is None:
        k_all, v_all, ki_all = k_new, v_new, ki_new
    else:
        p_len = k_past.shape[1]
        k_all = jnp.concatenate([k_past.reshape(bsz, p_len, kw), k_new], axis=1)
        v_all = jnp.concatenate([v_past.reshape(bsz, p_len, kw), v_new], axis=1)
        ki_all = jnp.concatenate([ki_past, ki_new], axis=1)
    s_real = k_all.shape[1]
    topk = min(TOPK_MAX, s_real // 4)

    tq = 256 if t_len % 256 == 0 else LANES
    tq_total = _round_up(t_len, tq)
    sk = 512
    s_pad = _round_up(s_real, sk)
    nkb = s_pad // sk
    pad_q = ((0, 0), (0, tq_total - t_len), (0, 0))
    pad_s = ((0, 0), (0, s_pad - s_real), (0, 0))
    grp = N_HEADS // N_KV_HEADS
    qg = jnp.pad(q, pad_q).reshape(bsz, tq_total // tq, tq, N_KV_HEADS, grp, HEAD_DIM)
    qg = qg.transpose(0, 1, 3, 5, 4, 2).reshape(bsz, tq_total // tq, N_KV_HEADS * HEAD_DIM, grp * tq)
    qiT = jnp.pad(qi, pad_q).transpose(0, 2, 1)
    wiT = jnp.pad(wi, pad_q).transpose(0, 2, 1)
    k_b = jnp.pad(k_all.astype(BF16), pad_s)
    ki_b = jnp.pad(ki_all.astype(BF16), pad_s)
    v_b = jnp.pad(v_all.astype(BF16), pad_s).reshape(bsz, nkb, sk, N_KV_HEADS, HEAD_DIM)
    vT = v_b.transpose(0, 1, 3, 4, 2)
    ones = jnp.ones((bsz, nkb, N_KV_HEADS, 1, sk), BF16)
    zeros = jnp.zeros((bsz, nkb, N_KV_HEADS, V_ROWS - HEAD_DIM - 1, sk), BF16)
    vT_blk = jnp.concatenate([vT, ones, zeros], axis=3).reshape(bsz, nkb, N_KV_HEADS * V_ROWS, sk)
    yb = _dsa_call(qg, qiT, wiT, k_b, vT_blk, ki_b, tq=tq, sk=sk, s_real=s_real,
                   q_pos0=pos0, n_q_real=t_len, topk=topk)[:, :t_len]

    n = bsz * t_len
    tm = 512 if n % 512 == 0 else n
    y = _ffn_call(x.reshape(n, d), ya.reshape(n, w), yb.reshape(n, N_HEADS * HEAD_DIM), wts, tm=tm)
    return (y.reshape(bsz, t_len, d),
            k_new.reshape(bsz, t_len, N_KV_HEADS, HEAD_DIM),
            v_new.reshape(bsz, t_len, N_KV_HEADS, HEAD_DIM),
            ki_new,
            hlast[:, SUBLANES - 1],
            tail[:, SUBLANES - (CONV_W - 1):])


def kernel(x_prompt, x_sample, cache_k, cache_v, cache_kidx, state_h, state_conv, norm_mix, w_in, conv_w, conv_b, w_rg, b_rg, w_ig, b_ig, lru_lambda, q_norm, k_norm, w_out, norm_ffn, w_ffn_in, w_ffn_out):
    wts = _prep_weights(norm_mix, w_in, conv_w, conv_b, w_rg, b_rg, w_ig, b_ig, lru_lambda,
                        q_norm, k_norm, w_out, norm_ffn, w_ffn_in, w_ffn_out)
    bp = x_prompt.shape[0]
    w = conv_b.shape[0]
    p_len = cache_k.shape[1]
    conv0 = jnp.zeros((bp, CONV_W - 1, w), x_prompt.dtype)
    h0 = jnp.zeros((bp, w), x_prompt.dtype)
    yp, k_p, v_p, ki_p, h_p, conv_p = _layer(x_prompt, 0, conv0, h0, None, None, None, wts)
    ys, k_s, v_s, ki_s, h_s, conv_s = _layer(x_sample, p_len, state_conv, state_h,
                                             cache_k, cache_v, cache_kidx, wts)
    return (yp, ys, k_p, v_p, ki_p, h_p, conv_p, k_s, v_s, ki_s, h_s, conv_s)
```

```python
import functools

import numpy as np
import jax
import jax.numpy as jnp
from jax import lax
from jax.experimental import pallas as pl
from jax.experimental.pallas import tpu as pltpu

F32 = jnp.float32
BF16 = jnp.bfloat16

CHUNK = 64
EPS = 1e-6
LRU_BLOCKS = 8
CONV_W = 4
LRU_C = 8.0
N_HEADS = 8
N_KV_HEADS = 2
HEAD_DIM = 64
N_IDX_HEADS = 8
IDX_DIM = 64
TOPK_MAX = 256
ROPE_THETA = 10000.0

LANES = 128
SUBLANES = 8
VMEM_LIMIT = 48 * 1024 * 1024
INT_MIN = -2 ** 31
INT_MAX = 2 ** 31 - 1
VALUE_SEARCH_PASSES = 26
NEG_BIG = -1e30
LOG2_E = float(np.log2(np.e))
K_AUG = LANES
M_SLACK = 1.0 + 2.0 ** -5
CNT_ROWS = 4 * SUBLANES
L_FLOOR = 2.0 ** -80
BF16_SUBLANES = 16
V_ROWS = HEAD_DIM + BF16_SUBLANES


def _cparams(sem):
    return pltpu.CompilerParams(dimension_semantics=sem, vmem_limit_bytes=VMEM_LIMIT)


def _const_spec(shape):
    nd = len(shape)
    return pl.BlockSpec(shape, lambda *_: (0,) * nd)


def _swap_halves(z):
    lane = lax.broadcasted_iota(jnp.int32, z.shape, 1)
    lo_half = (lane % HEAD_DIM) < (HEAD_DIM // 2)
    return jnp.where(lo_half, pltpu.roll(z, LANES - HEAD_DIM // 2, 1), pltpu.roll(z, HEAD_DIM // 2, 1))


def _rope(z, cos, sin):
    outs = []
    for s in range(z.shape[1] // LANES):
        zs = z[:, s * LANES:(s + 1) * LANES]
        outs.append(zs * cos + _swap_halves(zs) * sin)
    return outs[0] if len(outs) == 1 else jnp.concatenate(outs, axis=1)


def _head_rmsnorm(z, gain, g):
    z2 = z * z
    hi = z2.astype(BF16)
    lo = (z2 - hi.astype(F32)).astype(BF16)
    ss = jnp.dot(hi, g, preferred_element_type=F32) + jnp.dot(lo, g, preferred_element_type=F32)
    return z * lax.rsqrt(ss * (1.0 / HEAD_DIM) + EPS) * gain


def _gelu_tanh(x):
    return 0.5 * x * (1.0 + jnp.tanh(np.sqrt(2.0 / np.pi).astype(np.float32) * (x + 0.044715 * (x * x * x))))


def _proj_kernel(x_ref, cos_ref, sin_ref, cprev_ref, hprev_ref,
                 nmix_ref, wa_ref, wq_ref, wkv_ref, wqi_ref, wkw_ref,
                 convw_ref, convb_ref, wrg_ref, wig_ref, brg_ref, big_ref, lam_ref,
                 qn_ref, kn_ref, g_ref,
                 ya_ref, q_ref, qi_ref, kv_ref, kw_ref, tail_ref, hlast_ref,
                 cbuf, hcar, *, tb, pos0, wi_scale):
    t = pl.program_id(1)
    w = cbuf.shape[1]

    @pl.when(t == 0)
    def _():
        cbuf[0:SUBLANES, :] = cprev_ref[0]
        hcar[...] = hprev_ref[0]

    x = x_ref[0]
    ms = jnp.mean(x * x, axis=-1, keepdims=True)
    h = (x * lax.rsqrt(ms + EPS) * nmix_ref[...]).astype(BF16)

    pa = jnp.dot(h, wa_ref[...], preferred_element_type=F32)
    xr = pa[:, :w]
    gate = pa[:, w:]
    cbuf[SUBLANES:SUBLANES + tb, :] = xr
    xc = convb_ref[...]
    for j in range(CONV_W):
        off = SUBLANES - (CONV_W - 1) + j
        xc = xc + cbuf[off:off + tb, :] * convw_ref[j:j + 1, :]
    tail = cbuf[tb:tb + SUBLANES, :]
    cbuf[0:SUBLANES, :] = tail
    tail_ref[0] = tail

    xcb = xc.astype(BF16)
    r = jax.nn.sigmoid(jnp.dot(xcb, wrg_ref[...], preferred_element_type=F32) + brg_ref[...])
    ig = jax.nn.sigmoid(jnp.dot(xcb, wig_ref[...], preferred_element_type=F32) + big_ref[...])
    nl = -lam_ref[...]
    softplus = jnp.maximum(nl, 0.0) + jnp.log(1.0 + jnp.exp(-jnp.abs(nl)))
    log_a = (-LRU_C) * r * softplus
    a = jnp.exp(log_a)
    mult = jnp.sqrt(1.0 - jnp.exp(2.0 * log_a))
    row = lax.broadcasted_iota(jnp.int32, (tb, 1), 0)
    mult = jnp.where(row + (pos0 + t * tb) == 0, 1.0, mult)
    b = mult * ig * xc
    s = 1
    while s < tb:
        keep = row >= s
        a_s = jnp.where(keep, pltpu.roll(a, s, 0), 1.0)
        b_s = jnp.where(keep, pltpu.roll(b, s, 0), 0.0)
        b = a * b_s + b
        a = a * a_s
        s *= 2
    hs = a * hcar[...] + b
    hcar[...] = hs[tb - 1:tb, :]
    hlast_ref[0] = hs[tb - SUBLANES:tb, :]
    ya_ref[0] = (hs * _gelu_tanh(gate)).astype(ya_ref.dtype)

    cos = cos_ref[...]
    sin = sin_ref[...]
    g = g_ref[...]
    pq = jnp.dot(h, wq_ref[...], preferred_element_type=F32)
    q = _rope(_head_rmsnorm(pq, qn_ref[...], g), cos, sin)
    q_ref[0] = (q * (LOG2_E * HEAD_DIM ** -0.5)).astype(q_ref.dtype)

    pkv = jnp.dot(h, wkv_ref[...], preferred_element_type=F32)
    kw_ = N_KV_HEADS * HEAD_DIM
    k = _rope(_head_rmsnorm(pkv[:, :kw_], kn_ref[...], g[:kw_, :kw_]), cos, sin)
    kv_ref[0] = jnp.concatenate([k, pkv[:, kw_:]], axis=1)

    pqi = jnp.dot(h, wqi_ref[...], preferred_element_type=F32)
    qi_ref[0] = _rope(pqi, cos, sin).astype(qi_ref.dtype)

    pkw = jnp.dot(h, wkw_ref[...], preferred_element_type=F32)
    lane = lax.broadcasted_iota(jnp.int32, pkw.shape, 1)
    kw_ref[0] = jnp.where(lane < IDX_DIM, _rope(pkw, cos, sin), pkw * wi_scale)


def _proj_call(x, cos, sin, cprev, hprev, wts, *, tb, pos0):
    bsz, t_len, d = x.shape
    w = wts["conv_b"].shape[1]
    aw = N_HEADS * HEAD_DIM
    kvw = 2 * N_KV_HEADS * HEAD_DIM
    nt = t_len // tb
    wi_scale = float(N_IDX_HEADS ** -0.5 * IDX_DIM ** -0.5)
    names = ("norm_mix", "w_a", "w_q", "w_kv", "w_qi", "w_kw", "conv_w", "conv_b",
             "w_rg", "w_ig", "b_rg", "b_ig", "lam", "q_norm", "k_norm", "g")
    warrs = [wts[n] for n in names]
    in_specs = [
        pl.BlockSpec((1, tb, d), lambda b, t: (b, t, 0)),
        pl.BlockSpec((tb, LANES), lambda b, t: (t, 0)),
        pl.BlockSpec((tb, LANES), lambda b, t: (t, 0)),
        pl.BlockSpec((1, SUBLANES, w), lambda b, t: (b, 0, 0)),
        pl.BlockSpec((1, 1, w), lambda b, t: (b, 0, 0)),
    ] + [_const_spec(a.shape) for a in warrs]
    out_shape = (
        jax.ShapeDtypeStruct((bsz, t_len, w), BF16),
        jax.ShapeDtypeStruct((bsz, t_len, aw), BF16),
        jax.ShapeDtypeStruct((bsz, t_len, aw), BF16),
        jax.ShapeDtypeStruct((bsz, t_len, kvw), F32),
        jax.ShapeDtypeStruct((bsz, t_len, LANES), F32),
        jax.ShapeDtypeStruct((bsz, SUBLANES, w), F32),
        jax.ShapeDtypeStruct((bsz, SUBLANES, w), F32),
    )
    out_specs = (
        pl.BlockSpec((1, tb, w), lambda b, t: (b, t, 0)),
        pl.BlockSpec((1, tb, aw), lambda b, t: (b, t, 0)),
        pl.BlockSpec((1, tb, aw), lambda b, t: (b, t, 0)),
        pl.BlockSpec((1, tb, kvw), lambda b, t: (b, t, 0)),
        pl.BlockSpec((1, tb, LANES), lambda b, t: (b, t, 0)),
        pl.BlockSpec((1, SUBLANES, w), lambda b, t: (b, 0, 0)),
        pl.BlockSpec((1, SUBLANES, w), lambda b, t: (b, 0, 0)),
    )
    return pl.pallas_call(
        functools.partial(_proj_kernel, tb=tb, pos0=pos0, wi_scale=wi_scale),
        grid=(bsz, nt),
        in_specs=in_specs,
        out_specs=out_specs,
        out_shape=out_shape,
        scratch_shapes=[pltpu.VMEM((SUBLANES + tb, w), F32), pltpu.VMEM((1, w), F32)],
        compiler_params=_cparams(("arbitrary", "arbitrary")),
        name="proj",
    )(x, cos, sin, cprev, hprev, *warrs)


def _dsa_kernel(qT_ref, qiT_ref, wiT_ref, k_ref, vT_ref, ki_ref, kmax_ref, o_ref,
                keys_scr, m_scr, acc_scr, qa_scr, *, tq, sk, s_real, n_kblocks, q_pos0, n_q_real, topk):
    i = pl.program_id(1)
    q_idx = i * tq + lax.broadcasted_iota(jnp.int32, (1, tq), 1)
    q_adm_end = ((q_pos0 + q_idx) // CHUNK + 1) * CHUNK
    q_adm_end = jnp.minimum(q_adm_end, s_real)
    last_q = jnp.minimum(i * tq + tq - 1, n_q_real - 1)
    blk_end = jnp.minimum(((q_pos0 + last_q) // CHUNK + 1) * CHUNK, s_real)
    nkb = jnp.minimum((blk_end + sk - 1) // sk, n_kblocks)

    def to_key(x):
        bits = lax.bitcast_convert_type(x, jnp.int32)
        return jnp.where(bits < 0, INT_MIN - bits, bits)

    def to_val(key):
        return lax.bitcast_convert_type(jnp.where(key < 0, INT_MIN - key, key), F32)

    def score_body(j, carry):
        kmax, kmin = carry
        off = pl.multiple_of(j * sk, sk)
        kib = ki_ref[0, pl.ds(off, sk), :]
        acc = jnp.zeros((sk, tq), F32)
        for hh in range(N_IDX_HEADS):
            sc = jnp.dot(kib, qiT_ref[0, hh * IDX_DIM:(hh + 1) * IDX_DIM, :], preferred_element_type=F32)
            acc = acc + jnp.maximum(sc, 0.0) * wiT_ref[0, hh:hh + 1, :]
        key = to_key(acc)
        kidx = off + lax.broadcasted_iota(jnp.int32, (sk, 1), 0)
        adm = kidx < q_adm_end
        hi_part = jnp.where(adm, key, INT_MIN)
        lo_part = jnp.where(adm, key, INT_MAX)
        keys_scr[pl.ds(off, sk), :] = hi_part
        return (jnp.maximum(kmax, jnp.max(hi_part.reshape(sk // CNT_ROWS, CNT_ROWS, tq), axis=0)),
                jnp.minimum(kmin, jnp.min(lo_part.reshape(sk // CNT_ROWS, CNT_ROWS, tq), axis=0)))

    kmax, kmin = lax.fori_loop(
        0, nkb, score_body,
        (jnp.full((CNT_ROWS, tq), INT_MIN, jnp.int32), jnp.full((CNT_ROWS, tq), INT_MAX, jnp.int32)))
    kmax = jnp.max(kmax, axis=0, keepdims=True)
    kmin = jnp.min(kmin, axis=0, keepdims=True)

    def count_ge(thr):
        def body(j, cnt):
            off = pl.multiple_of(j * sk, sk)
            ge = (keys_scr[pl.ds(off, sk), :] >= thr).astype(jnp.int32)
            return cnt + jnp.sum(ge.reshape(sk // CNT_ROWS, CNT_ROWS, tq), axis=0)
        cnt = lax.fori_loop(0, nkb, body, jnp.zeros((CNT_ROWS, tq), jnp.int32))
        return jnp.sum(cnt, axis=0, keepdims=True)

    open0 = jnp.logical_and(q_adm_end > topk, q_idx < n_q_real).astype(jnp.int32)

    def vs_cond(st):
        it, _, _, _, _, n_open = st
        return jnp.logical_and(it < VALUE_SEARCH_PASSES, n_open > 0)

    def vs_body(st):
        it, lo_k, hi_k, thr_v, opn, _ = st
        lo_f = to_val(lo_k)
        mid_k = to_key(lo_f + (to_val(hi_k) - lo_f) * 0.5)
        mid_k = jnp.minimum(jnp.maximum(mid_k, lo_k + 1), hi_k)
        cnt = count_ge(mid_k)
        is_open = opn > 0
        hit = jnp.logical_and(is_open, cnt == topk)
        thr_v = jnp.where(hit, mid_k, thr_v)
        lo_k = jnp.where(jnp.logical_and(is_open, cnt > topk), mid_k, lo_k)
        hi_k = jnp.where(jnp.logical_and(is_open, cnt < topk), mid_k, hi_k)
        opn = jnp.where(hit, 0, opn)
        return it + 1, lo_k, hi_k, thr_v, opn, jnp.sum(opn)

    thr_v0 = jnp.full((1, tq), INT_MIN + 1, jnp.int32)
    _, _, _, thr_v, open1, n_open1 = lax.while_loop(
        vs_cond, vs_body, (jnp.int32(0), kmin, kmax, thr_v0, open0, jnp.sum(open0)))

    tu0 = jnp.zeros((1, tq), jnp.int32)

    def search_cond(st):
        bit, _, _, n_open = st
        return jnp.logical_and(bit >= 0, n_open > 0)

    def search_body(st):
        bit, tu, opn, _ = st
        cand = tu | lax.shift_left(jnp.int32(1), bit)
        cnt = count_ge(cand ^ INT_MIN)
        accept = jnp.logical_and(cnt >= topk, opn > 0)
        tu = jnp.where(accept, cand, tu)
        opn = jnp.where(jnp.logical_and(accept, cnt == topk), 0, opn)
        return bit - 1, tu, opn, jnp.sum(opn)

    _, tu, opn, n_open = lax.while_loop(search_cond, search_body, (jnp.int32(31), tu0, open1, n_open1))
    thr = jnp.where(open1 > 0, jnp.maximum(tu ^ INT_MIN, INT_MIN + 1), thr_v)

    @pl.when(n_open > 0)
    def _():
        cnt_gt = count_ge(thr + 1)
        need = jnp.where(opn > 0, topk - cnt_gt, jnp.int32(2 ** 30)).astype(F32)
        r_i = lax.broadcasted_iota(jnp.int32, (sk, sk), 0)
        c_i = lax.broadcasted_iota(jnp.int32, (sk, sk), 1)
        tri = (c_i <= r_i).astype(BF16)

        def tie_body(j, seen):
            off = pl.multiple_of(j * sk, sk)
            key = keys_scr[pl.ds(off, sk), :]
            eq = key == thr
            rank = jnp.dot(tri, eq.astype(BF16), preferred_element_type=F32) + seen
            keys_scr[pl.ds(off, sk), :] = jnp.where(jnp.logical_and(eq, rank > need), INT_MIN, key)
            return rank[sk - 1:sk, :]

        lax.fori_loop(0, nkb, tie_body, jnp.zeros((1, tq), F32))

    grp = N_HEADS // N_KV_HEADS

    def finish():
        outs = []
        for gk in range(N_KV_HEADS):
            acc = acc_scr[gk]
            for hq in range(grp):
                sl = slice(hq * tq, (hq + 1) * tq)
                outs.append(acc[:HEAD_DIM, sl] / acc[HEAD_DIM:HEAD_DIM + 1, sl])
        o_ref[0] = jnp.concatenate(outs, axis=0).T.astype(o_ref.dtype)

    for gk in range(N_KV_HEADS):
        qb = qT_ref[0, 0, gk * HEAD_DIM:(gk + 1) * HEAD_DIM, :]
        qf = qb.astype(F32)
        qn2 = jnp.sum(qf * qf, axis=0, keepdims=True)
        m = jnp.sqrt(qn2 * kmax_ref[0, :, gk * HEAD_DIM:gk * HEAD_DIM + 1]) * M_SLACK
        row = lax.broadcasted_iota(jnp.int32, (BF16_SUBLANES, grp * tq), 0)
        qa_scr[gk, 0:HEAD_DIM, :] = qb
        qa_scr[gk, HEAD_DIM:HEAD_DIM + BF16_SUBLANES, :] = jnp.where(row == 0, -m, 0.0).astype(BF16)
        qa_scr[gk, HEAD_DIM + BF16_SUBLANES:, :] = jnp.zeros(
            (K_AUG - HEAD_DIM - BF16_SUBLANES, grp * tq), BF16)
    acc_scr[...] = jnp.zeros(acc_scr.shape, F32)

    def fast_body(j, carry):
        off = pl.multiple_of(j * sk, sk)
        sel = keys_scr[pl.ds(off, sk), :] >= thr
        logits = [jnp.dot(k_ref[0, pl.ds(off, sk), gk * K_AUG:(gk + 1) * K_AUG], qa_scr[gk],
                          preferred_element_type=F32) for gk in range(N_KV_HEADS)]
        for gk in range(N_KV_HEADS):
            s = logits[gk]
            p = jnp.concatenate(
                [jnp.exp2(jnp.where(sel, s[:, hq * tq:(hq + 1) * tq], NEG_BIG)) for hq in range(grp)],
                axis=1).astype(BF16)
            vtb = vT_ref[0, j, gk * V_ROWS:(gk + 1) * V_ROWS, :]
            acc_scr[gk] += jnp.dot(vtb, p, preferred_element_type=F32)
        return carry

    lax.fori_loop(0, nkb, fast_body, 0)
    l_min = jnp.min(acc_scr[:, HEAD_DIM:HEAD_DIM + 1, :])
    healthy = l_min > L_FLOOR

    @pl.when(healthy)
    def _():
        finish()

    @pl.when(jnp.logical_not(healthy))
    def _():
        m_scr[...] = jnp.full(m_scr.shape, NEG_BIG, F32)
        acc_scr[...] = jnp.zeros(acc_scr.shape, F32)

        def slow_body(j, carry):
            off = pl.multiple_of(j * sk, sk)
            sel = keys_scr[pl.ds(off, sk), :] >= thr
            for gk in range(N_KV_HEADS):
                kb = k_ref[0, pl.ds(off, sk), gk * K_AUG:gk * K_AUG + HEAD_DIM]
                vtb = vT_ref[0, j, gk * V_ROWS:(gk + 1) * V_ROWS, :]
                s = jnp.dot(kb, qT_ref[0, 0, gk * HEAD_DIM:(gk + 1) * HEAD_DIM, :],
                            preferred_element_type=F32)
                s = jnp.concatenate(
                    [jnp.where(sel, s[:, hq * tq:(hq + 1) * tq], NEG_BIG) for hq in range(grp)], axis=1)
                m_old = m_scr[gk]
                m_new = jnp.maximum(m_old, jnp.max(s, axis=0, keepdims=True))
                p = jnp.exp2(s - m_new).astype(BF16)
                alpha = jnp.exp2(m_old - m_new)
                acc_scr[gk] = alpha * acc_scr[gk] + jnp.dot(vtb, p, preferred_element_type=F32)
                m_scr[gk] = m_new
            return carry

        lax.fori_loop(0, nkb, slow_body, 0)
        finish()


def _kmax_kernel(k_ref, g_ref, o_ref):
    kf = k_ref[0].astype(F32)
    ss = jnp.dot((kf * kf).astype(BF16), g_ref[...], preferred_element_type=F32)
    o_ref[0] = jnp.max(ss, axis=0, keepdims=True)


def _kmax_call(k_b, g):
    bsz, s_pad, kw = k_b.shape
    return pl.pallas_call(
        _kmax_kernel,
        grid=(bsz,),
        in_specs=[pl.BlockSpec((1, s_pad, kw), lambda b: (b, 0, 0)), _const_spec(g.shape)],
        out_specs=pl.BlockSpec((1, 1, kw), lambda b: (b, 0, 0)),
        out_shape=jax.ShapeDtypeStruct((bsz, 1, kw), F32),
        compiler_params=_cparams(("arbitrary",)),
        name="kmax",
    )(k_b, g)


def _dsa_call(qg, qiT, wiT, k_aug, vT_blk, ki_b, kmax, *, tq, sk, s_real, q_pos0, n_q_real, topk):
    bsz, aw, tq_total = qiT.shape
    n_kblocks = vT_blk.shape[1]
    s_pad = k_aug.shape[1]
    nq = tq_total // tq
    grp = N_HEADS // N_KV_HEADS
    kern = functools.partial(_dsa_kernel, tq=tq, sk=sk, s_real=s_real, n_kblocks=n_kblocks,
                             q_pos0=q_pos0, n_q_real=n_q_real, topk=topk)
    return pl.pallas_call(
        kern,
        grid=(bsz, nq),
        in_specs=[
            pl.BlockSpec((1, 1, N_KV_HEADS * HEAD_DIM, grp * tq), lambda b, i: (b, i, 0, 0)),
            pl.BlockSpec((1, aw, tq), lambda b, i: (b, 0, i)),
            pl.BlockSpec((1, N_IDX_HEADS, tq), lambda b, i: (b, 0, i)),
            pl.BlockSpec((1, s_pad, N_KV_HEADS * K_AUG), lambda b, i: (b, 0, 0)),
            pl.BlockSpec((1, n_kblocks, N_KV_HEADS * V_ROWS, sk), lambda b, i: (b, 0, 0, 0)),
            pl.BlockSpec((1, s_pad, IDX_DIM), lambda b, i: (b, 0, 0)),
            pl.BlockSpec((1, 1, N_KV_HEADS * HEAD_DIM), lambda b, i: (b, 0, 0)),
        ],
        out_specs=pl.BlockSpec((1, tq, aw), lambda b, i: (b, i, 0)),
        out_shape=jax.ShapeDtypeStruct((bsz, tq_total, aw), BF16),
        scratch_shapes=[
            pltpu.VMEM((s_pad, tq), jnp.int32),
            pltpu.VMEM((N_KV_HEADS, 1, grp * tq), F32),
            pltpu.VMEM((N_KV_HEADS, V_ROWS, grp * tq), F32),
            pltpu.VMEM((N_KV_HEADS, K_AUG, grp * tq), BF16),
        ],
        compiler_params=_cparams(("arbitrary", "arbitrary")),
        name="dsa",
    )(qg, qiT, wiT, k_aug, vT_blk, ki_b, kmax)


def _ffn_kernel(x_ref, ya_ref, yb_ref, woa_ref, wob_ref, nf_ref, wg_ref, wu_ref, wo_ref, o_ref, *, n_chunks):
    x1 = x_ref[...] + jnp.dot(ya_ref[...], woa_ref[...], preferred_element_type=F32)
    x1 = x1 + jnp.dot(yb_ref[...], wob_ref[...], preferred_element_type=F32)
    ms = jnp.mean(x1 * x1, axis=-1, keepdims=True)
    hf = (x1 * lax.rsqrt(ms + EPS) * nf_ref[...]).astype(BF16)

    def body(c, acc):
        gt = jnp.dot(hf, wg_ref[c], preferred_element_type=F32)
        up = jnp.dot(hf, wu_ref[c], preferred_element_type=F32)
        act = (gt * jax.nn.sigmoid(gt) * up).astype(BF16)
        return acc + jnp.dot(act, wo_ref[c], preferred_element_type=F32)

    o_ref[...] = lax.fori_loop(0, n_chunks, body, x1)


def _ffn_call(x2, ya2, yb2, wts, *, tm):
    n, d = x2.shape
    w = ya2.shape[1]
    aw = yb2.shape[1]
    n_chunks, _, fc = wts["w_g"].shape
    single = pl.Buffered(1)

    def wspec(shape):
        nd = len(shape)
        return pl.BlockSpec(shape, lambda *_: (0,) * nd, pipeline_mode=single)

    return pl.pallas_call(
        functools.partial(_ffn_kernel, n_chunks=n_chunks),
        grid=(n // tm,),
        in_specs=[
            pl.BlockSpec((tm, d), lambda r: (r, 0)),
            pl.BlockSpec((tm, w), lambda r: (r, 0)),
            pl.BlockSpec((tm, aw), lambda r: (r, 0)),
            wspec(wts["w_out_a"].shape), wspec(wts["w_out_b"].shape), wspec(wts["norm_ffn"].shape),
            wspec(wts["w_g"].shape), wspec(wts["w_u"].shape), wspec(wts["w_o"].shape),
        ],
        out_specs=pl.BlockSpec((tm, d), lambda r: (r, 0)),
        out_shape=jax.ShapeDtypeStruct((n, d), F32),
        compiler_params=_cparams(("arbitrary",)),
        name="ffn",
    )(x2, ya2, yb2, wts["w_out_a"], wts["w_out_b"], wts["norm_ffn"], wts["w_g"], wts["w_u"], wts["w_o"])


def _block_diag(wb):
    n, a, b = wb.shape
    eye = jnp.eye(n, dtype=wb.dtype)
    return (eye[:, None, :, None] * wb[:, :, None, :]).reshape(n * a, n * b)


def _ffn_chunk(d_ff):
    for c in (512, 256, 128):
        if d_ff % c == 0:
            return c
    return d_ff


def _prep_weights(norm_mix, w_in, conv_w, conv_b, w_rg, b_rg, w_ig, b_ig, lru_lambda,
                  q_norm, k_norm, w_out, norm_ffn, w_ffn_in, w_ffn_out):
    d = w_in.shape[0]
    w = conv_b.shape[0]
    aw = N_HEADS * HEAD_DIM
    kw = N_KV_HEADS * HEAD_DIM
    iw = N_IDX_HEADS * IDX_DIM
    o = np.cumsum([0, w, w, aw, kw, kw, iw, IDX_DIM, N_IDX_HEADS])
    wb = w_in.astype(BF16)
    w_kw = jnp.pad(wb[:, o[6]:o[8]], ((0, 0), (0, LANES - (IDX_DIM + N_IDX_HEADS))))
    d_ff = w_ffn_out.shape[0]
    fc = _ffn_chunk(d_ff)
    nck = d_ff // fc
    wfi = w_ffn_in.astype(BF16)
    hd_id = np.arange(aw) // HEAD_DIM
    return {
        "norm_mix": norm_mix.reshape(1, d),
        "w_a": wb[:, o[0]:o[2]],
        "w_q": wb[:, o[2]:o[3]],
        "w_kv": wb[:, o[3]:o[5]],
        "w_qi": wb[:, o[5]:o[6]],
        "w_kw": w_kw,
        "conv_w": jnp.pad(conv_w, ((0, SUBLANES - CONV_W), (0, 0))),
        "conv_b": conv_b.reshape(1, w),
        "w_rg": _block_diag(w_rg).astype(BF16),
        "w_ig": _block_diag(w_ig).astype(BF16),
        "b_rg": b_rg.reshape(1, w),
        "b_ig": b_ig.reshape(1, w),
        "lam": lru_lambda.reshape(1, w),
        "q_norm": jnp.tile(q_norm, N_HEADS).reshape(1, aw),
        "k_norm": jnp.tile(k_norm, N_KV_HEADS).reshape(1, kw),
        "g": jnp.asarray(hd_id[:, None] == hd_id[None, :], BF16),
        "w_out_a": w_out[:w].astype(BF16),
        "w_out_b": w_out[w:].astype(BF16),
        "norm_ffn": norm_ffn.reshape(1, d),
        "w_g": wfi[:, :d_ff].reshape(d, nck, fc).transpose(1, 0, 2),
        "w_u": wfi[:, d_ff:].reshape(d, nck, fc).transpose(1, 0, 2),
        "w_o": w_ffn_out.astype(BF16).reshape(nck, fc, d),
    }


def _rope_tables(pos):
    half = HEAD_DIM // 2
    inv = ROPE_THETA ** (-jnp.arange(half, dtype=F32) / half)
    ang = pos.astype(F32)[:, None] * inv[None, :]
    cos = jnp.cos(ang)
    sin = jnp.sin(ang)
    reps = LANES // HEAD_DIM
    return (jnp.tile(jnp.concatenate([cos, cos], axis=1), (1, reps)),
            jnp.tile(jnp.concatenate([-sin, sin], axis=1), (1, reps)))


def _round_up(n, m):
    return (n + m - 1) // m * m


def _layer(x, pos0, conv_prev, h_prev, k_past, v_past, ki_past, wts):
    bsz, t_len, d = x.shape
    w = conv_prev.shape[2]
    kw = N_KV_HEADS * HEAD_DIM
    pos = pos0 + jnp.arange(t_len, dtype=jnp.int32)
    cos, sin = _rope_tables(pos)
    tb = 256 if t_len % 256 == 0 else t_len
    cprev = jnp.pad(conv_prev, ((0, 0), (SUBLANES - (CONV_W - 1), 0), (0, 0)))
    ya, q, qi, kv, kwi, tail, hlast = _proj_call(
        x, cos, sin, cprev, h_prev.reshape(bsz, 1, w), wts, tb=tb, pos0=pos0)
    k_new = kv[..., :kw]
    v_new = kv[..., kw:]
    ki_new = kwi[..., :IDX_DIM]
    wi = kwi[..., IDX_DIM:IDX_DIM + N_IDX_HEADS]

    if k_past is None:
        k_all, v_all, ki_all = k_new, v_new, ki_new
    else:
        p_len = k_past.shape[1]
        k_all = jnp.concatenate([k_past.reshape(bsz, p_len, kw), k_new], axis=1)
        v_all = jnp.concatenate([v_past.reshape(bsz, p_len, kw), v_new], axis=1)
        ki_all = jnp.concatenate([ki_past, ki_new], axis=1)
    s_real = k_all.shape[1]
    topk = min(TOPK_MAX, s_real // 4)

    tq = 256 if t_len % 256 == 0 else LANES
    tq_total = _round_up(t_len, tq)
    sk = 512
    s_pad = _round_up(s_real, sk)
    nkb = s_pad // sk
    pad_q = ((0, 0), (0, tq_total - t_len), (0, 0))
    pad_s = ((0, 0), (0, s_pad - s_real), (0, 0))
    grp = N_HEADS // N_KV_HEADS
    qg = jnp.pad(q, pad_q).reshape(bsz, tq_total // tq, tq, N_KV_HEADS, grp, HEAD_DIM)
    qg = qg.transpose(0, 1, 3, 5, 4, 2).reshape(bsz, tq_total // tq, N_KV_HEADS * HEAD_DIM, grp * tq)
    qiT = jnp.pad(qi, pad_q).transpose(0, 2, 1)
    wiT = jnp.pad(wi, pad_q).transpose(0, 2, 1)
    k_b = jnp.pad(k_all.astype(BF16), pad_s)
    ki_b = jnp.pad(ki_all.astype(BF16), pad_s)
    v_b = jnp.pad(v_all.astype(BF16), pad_s).reshape(bsz, nkb, sk, N_KV_HEADS, HEAD_DIM)
    vT = v_b.transpose(0, 1, 3, 4, 2)
    ones = jnp.ones((bsz, nkb, N_KV_HEADS, 1, sk), BF16)
    zeros = jnp.zeros((bsz, nkb, N_KV_HEADS, V_ROWS - HEAD_DIM - 1, sk), BF16)
    vT_blk = jnp.concatenate([vT, ones, zeros], axis=3).reshape(bsz, nkb, N_KV_HEADS * V_ROWS, sk)
    kmax = _kmax_call(k_b, wts["g"][:kw, :kw])
    k4 = k_b.reshape(bsz, s_pad, N_KV_HEADS, HEAD_DIM)
    k_aug = jnp.concatenate(
        [k4, jnp.ones((bsz, s_pad, N_KV_HEADS, 1), BF16),
         jnp.zeros((bsz, s_pad, N_KV_HEADS, K_AUG - HEAD_DIM - 1), BF16)], axis=3,
    ).reshape(bsz, s_pad, N_KV_HEADS * K_AUG)
    yb = _dsa_call(qg, qiT, wiT, k_aug, vT_blk, ki_b, kmax, tq=tq, sk=sk, s_real=s_real,
                   q_pos0=pos0, n_q_real=t_len, topk=topk)[:, :t_len]

    n = bsz * t_len
    tm = 512 if n % 512 == 0 else n
    y = _ffn_call(x.reshape(n, d), ya.reshape(n, w), yb.reshape(n, N_HEADS * HEAD_DIM), wts, tm=tm)
    return (y.reshape(bsz, t_len, d),
            k_new.reshape(bsz, t_len, N_KV_HEADS, HEAD_DIM),
            v_new.reshape(bsz, t_len, N_KV_HEADS, HEAD_DIM),
            ki_new,
            hlast[:, SUBLANES - 1],
            tail[:, SUBLANES - (CONV_W - 1):])


def kernel(x_prompt, x_sample, cache_k, cache_v, cache_kidx, state_h, state_conv, norm_mix, w_in, conv_w, conv_b, w_rg, b_rg, w_ig, b_ig, lru_lambda, q_norm, k_norm, w_out, norm_ffn, w_ffn_in, w_ffn_out):
    wts = _prep_weights(norm_mix, w_in, conv_w, conv_b, w_rg, b_rg, w_ig, b_ig, lru_lambda,
                        q_norm, k_norm, w_out, norm_ffn, w_ffn_in, w_ffn_out)
    bp = x_prompt.shape[0]
    w = conv_b.shape[0]
    p_len = cache_k.shape[1]
    conv0 = jnp.zeros((bp, CONV_W - 1, w), x_prompt.dtype)
    h0 = jnp.zeros((bp, w), x_prompt.dtype)
    yp, k_p, v_p, ki_p, h_p, conv_p = _layer(x_prompt, 0, conv0, h0, None, None, None, wts)
    ys, k_s, v_s, ki_s, h_s, conv_s = _layer(x_sample, p_len, state_conv, state_h,
                                             cache_k, cache_v, cache_kidx, wts)
    return (yp, ys, k_p, v_p, ki_p, h_p, conv_p, k_s, v_s, ki_s, h_s, conv_s)
```

```python
import functools

import numpy as np
import jax
import jax.numpy as jnp
from jax import lax
from jax.experimental import pallas as pl
from jax.experimental.pallas import tpu as pltpu

F32 = jnp.float32
BF16 = jnp.bfloat16

CHUNK = 64
EPS = 1e-6
LRU_BLOCKS = 8
CONV_W = 4
LRU_C = 8.0
N_HEADS = 8
N_KV_HEADS = 2
HEAD_DIM = 64
N_IDX_HEADS = 8
IDX_DIM = 64
TOPK_MAX = 256
ROPE_THETA = 10000.0

LANES = 128
SUBLANES = 8
VMEM_LIMIT = 48 * 1024 * 1024
FLT_MAX = float(np.finfo(np.float32).max)
FLT_TINY = float(np.finfo(np.float32).tiny)
SEARCH_PASS_CAP = 512
NEG_BIG = -1e30
LOG2_E = float(np.log2(np.e))
K_AUG = LANES
M_SLACK = 1.0 + 2.0 ** -5
CNT_ROWS = 4 * SUBLANES
L_FLOOR = 2.0 ** -80
BF16_SUBLANES = 16
V_ROWS = HEAD_DIM + BF16_SUBLANES


def _cparams(sem):
    return pltpu.CompilerParams(dimension_semantics=sem, vmem_limit_bytes=VMEM_LIMIT)


def _const_spec(shape):
    nd = len(shape)
    return pl.BlockSpec(shape, lambda *_: (0,) * nd)


def _swap_halves(z):
    lane = lax.broadcasted_iota(jnp.int32, z.shape, 1)
    lo_half = (lane % HEAD_DIM) < (HEAD_DIM // 2)
    return jnp.where(lo_half, pltpu.roll(z, LANES - HEAD_DIM // 2, 1), pltpu.roll(z, HEAD_DIM // 2, 1))


def _rope(z, cos, sin):
    outs = []
    for s in range(z.shape[1] // LANES):
        zs = z[:, s * LANES:(s + 1) * LANES]
        outs.append(zs * cos + _swap_halves(zs) * sin)
    return outs[0] if len(outs) == 1 else jnp.concatenate(outs, axis=1)


def _head_rmsnorm(z, gain, g):
    z2 = z * z
    hi = z2.astype(BF16)
    lo = (z2 - hi.astype(F32)).astype(BF16)
    ss = jnp.dot(hi, g, preferred_element_type=F32) + jnp.dot(lo, g, preferred_element_type=F32)
    return z * lax.rsqrt(ss * (1.0 / HEAD_DIM) + EPS) * gain


def _gelu_tanh(x):
    return 0.5 * x * (1.0 + jnp.tanh(np.sqrt(2.0 / np.pi).astype(np.float32) * (x + 0.044715 * (x * x * x))))


def _proj_kernel(x_ref, cos_ref, sin_ref, cprev_ref, hprev_ref,
                 nmix_ref, wa_ref, wq_ref, wkv_ref, wqi_ref, wkw_ref,
                 convw_ref, convb_ref, wrg_ref, wig_ref, brg_ref, big_ref, lam_ref,
                 qn_ref, kn_ref, g_ref,
                 ya_ref, q_ref, qi_ref, kv_ref, kw_ref, tail_ref, hlast_ref,
                 cbuf, hcar, *, tb, pos0, wi_scale):
    t = pl.program_id(1)
    w = cbuf.shape[1]

    @pl.when(t == 0)
    def _():
        cbuf[0:SUBLANES, :] = cprev_ref[0]
        hcar[...] = hprev_ref[0]

    x = x_ref[0]
    ms = jnp.mean(x * x, axis=-1, keepdims=True)
    h = (x * lax.rsqrt(ms + EPS) * nmix_ref[...]).astype(BF16)

    pa = jnp.dot(h, wa_ref[...], preferred_element_type=F32)
    xr = pa[:, :w]
    gate = pa[:, w:]
    cbuf[SUBLANES:SUBLANES + tb, :] = xr
    xc = convb_ref[...]
    for j in range(CONV_W):
        off = SUBLANES - (CONV_W - 1) + j
        xc = xc + cbuf[off:off + tb, :] * convw_ref[j:j + 1, :]
    tail = cbuf[tb:tb + SUBLANES, :]
    cbuf[0:SUBLANES, :] = tail
    tail_ref[0] = tail

    xcb = xc.astype(BF16)
    r = jax.nn.sigmoid(jnp.dot(xcb, wrg_ref[...], preferred_element_type=F32) + brg_ref[...])
    ig = jax.nn.sigmoid(jnp.dot(xcb, wig_ref[...], preferred_element_type=F32) + big_ref[...])
    nl = -lam_ref[...]
    softplus = jnp.maximum(nl, 0.0) + jnp.log(1.0 + jnp.exp(-jnp.abs(nl)))
    log_a = (-LRU_C) * r * softplus
    a = jnp.exp(log_a)
    mult = jnp.sqrt(1.0 - jnp.exp(2.0 * log_a))
    row = lax.broadcasted_iota(jnp.int32, (tb, 1), 0)
    mult = jnp.where(row + (pos0 + t * tb) == 0, 1.0, mult)
    b = mult * ig * xc
    s = 1
    while s < tb:
        keep = row >= s
        a_s = jnp.where(keep, pltpu.roll(a, s, 0), 1.0)
        b_s = jnp.where(keep, pltpu.roll(b, s, 0), 0.0)
        b = a * b_s + b
        a = a * a_s
        s *= 2
    hs = a * hcar[...] + b
    hcar[...] = hs[tb - 1:tb, :]
    hlast_ref[0] = hs[tb - SUBLANES:tb, :]
    ya_ref[0] = (hs * _gelu_tanh(gate)).astype(ya_ref.dtype)

    cos = cos_ref[...]
    sin = sin_ref[...]
    g = g_ref[...]
    pq = jnp.dot(h, wq_ref[...], preferred_element_type=F32)
    q = _rope(_head_rmsnorm(pq, qn_ref[...], g), cos, sin)
    q_ref[0] = (q * (LOG2_E * HEAD_DIM ** -0.5)).astype(q_ref.dtype)

    pkv = jnp.dot(h, wkv_ref[...], preferred_element_type=F32)
    kw_ = N_KV_HEADS * HEAD_DIM
    k = _rope(_head_rmsnorm(pkv[:, :kw_], kn_ref[...], g[:kw_, :kw_]), cos, sin)
    kv_ref[0] = jnp.concatenate([k, pkv[:, kw_:]], axis=1)

    pqi = jnp.dot(h, wqi_ref[...], preferred_element_type=F32)
    qi_ref[0] = _rope(pqi, cos, sin).astype(qi_ref.dtype)

    pkw = jnp.dot(h, wkw_ref[...], preferred_element_type=F32)
    lane = lax.broadcasted_iota(jnp.int32, pkw.shape, 1)
    kw_ref[0] = jnp.where(lane < IDX_DIM, _rope(pkw, cos, sin), pkw * wi_scale)


def _proj_call(x, cos, sin, cprev, hprev, wts, *, tb, pos0):
    bsz, t_len, d = x.shape
    w = wts["conv_b"].shape[1]
    aw = N_HEADS * HEAD_DIM
    kvw = 2 * N_KV_HEADS * HEAD_DIM
    nt = t_len // tb
    wi_scale = float(N_IDX_HEADS ** -0.5 * IDX_DIM ** -0.5)
    names = ("norm_mix", "w_a", "w_q", "w_kv", "w_qi", "w_kw", "conv_w", "conv_b",
             "w_rg", "w_ig", "b_rg", "b_ig", "lam", "q_norm", "k_norm", "g")
    warrs = [wts[n] for n in names]
    in_specs = [
        pl.BlockSpec((1, tb, d), lambda b, t: (b, t, 0)),
        pl.BlockSpec((tb, LANES), lambda b, t: (t, 0)),
        pl.BlockSpec((tb, LANES), lambda b, t: (t, 0)),
        pl.BlockSpec((1, SUBLANES, w), lambda b, t: (b, 0, 0)),
        pl.BlockSpec((1, 1, w), lambda b, t: (b, 0, 0)),
    ] + [_const_spec(a.shape) for a in warrs]
    out_shape = (
        jax.ShapeDtypeStruct((bsz, t_len, w), BF16),
        jax.ShapeDtypeStruct((bsz, t_len, aw), BF16),
        jax.ShapeDtypeStruct((bsz, t_len, aw), BF16),
        jax.ShapeDtypeStruct((bsz, t_len, kvw), F32),
        jax.ShapeDtypeStruct((bsz, t_len, LANES), F32),
        jax.ShapeDtypeStruct((bsz, SUBLANES, w), F32),
        jax.ShapeDtypeStruct((bsz, SUBLANES, w), F32),
    )
    out_specs = (
        pl.BlockSpec((1, tb, w), lambda b, t: (b, t, 0)),
        pl.BlockSpec((1, tb, aw), lambda b, t: (b, t, 0)),
        pl.BlockSpec((1, tb, aw), lambda b, t: (b, t, 0)),
        pl.BlockSpec((1, tb, kvw), lambda b, t: (b, t, 0)),
        pl.BlockSpec((1, tb, LANES), lambda b, t: (b, t, 0)),
        pl.BlockSpec((1, SUBLANES, w), lambda b, t: (b, 0, 0)),
        pl.BlockSpec((1, SUBLANES, w), lambda b, t: (b, 0, 0)),
    )
    return pl.pallas_call(
        functools.partial(_proj_kernel, tb=tb, pos0=pos0, wi_scale=wi_scale),
        grid=(bsz, nt),
        in_specs=in_specs,
        out_specs=out_specs,
        out_shape=out_shape,
        scratch_shapes=[pltpu.VMEM((SUBLANES + tb, w), F32), pltpu.VMEM((1, w), F32)],
        compiler_params=_cparams(("arbitrary", "arbitrary")),
        name="proj",
    )(x, cos, sin, cprev, hprev, *warrs)


def _dsa_kernel(qT_ref, qiT_ref, wiT_ref, k_ref, vT_ref, ki_ref, kmax_ref, o_ref,
                sc_scr, m_scr, acc_scr, qa_scr, *, tq, sk, s_real, n_kblocks, q_pos0, n_q_real, topk):
    i = pl.program_id(1)
    q_idx = i * tq + lax.broadcasted_iota(jnp.int32, (1, tq), 1)
    q_adm_end = ((q_pos0 + q_idx) // CHUNK + 1) * CHUNK
    q_adm_end = jnp.minimum(q_adm_end, s_real)
    last_q = jnp.minimum(i * tq + tq - 1, n_q_real - 1)
    blk_end = jnp.minimum(((q_pos0 + last_q) // CHUNK + 1) * CHUNK, s_real)
    nkb = jnp.minimum((blk_end + sk - 1) // sk, n_kblocks)

    def score_body(j, carry):
        smax, smin = carry
        off = pl.multiple_of(j * sk, sk)
        kib = ki_ref[0, pl.ds(off, sk), :]
        acc = jnp.zeros((sk, tq), F32)
        for hh in range(N_IDX_HEADS):
            sc = jnp.dot(kib, qiT_ref[0, hh * IDX_DIM:(hh + 1) * IDX_DIM, :], preferred_element_type=F32)
            acc = acc + jnp.maximum(sc, 0.0) * wiT_ref[0, hh:hh + 1, :]
        kidx = off + lax.broadcasted_iota(jnp.int32, (sk, 1), 0)
        adm = kidx < q_adm_end
        hi_part = jnp.where(adm, acc, -jnp.inf)
        lo_part = jnp.where(adm, acc, jnp.inf)
        sc_scr[pl.ds(off, sk), :] = hi_part
        return (jnp.maximum(smax, jnp.max(hi_part.reshape(sk // CNT_ROWS, CNT_ROWS, tq), axis=0)),
                jnp.minimum(smin, jnp.min(lo_part.reshape(sk // CNT_ROWS, CNT_ROWS, tq), axis=0)))

    smax, smin = lax.fori_loop(
        0, nkb, score_body,
        (jnp.full((CNT_ROWS, tq), -jnp.inf, F32), jnp.full((CNT_ROWS, tq), jnp.inf, F32)))
    smax = jnp.max(smax, axis=0, keepdims=True)
    smin = jnp.min(smin, axis=0, keepdims=True)

    def count(thr, strict=False):
        def body(j, cnt):
            off = pl.multiple_of(j * sk, sk)
            blk = sc_scr[pl.ds(off, sk), :]
            above = (blk > thr if strict else blk >= thr).astype(jnp.int32)
            return cnt + jnp.sum(above.reshape(sk // CNT_ROWS, CNT_ROWS, tq), axis=0)
        cnt = lax.fori_loop(0, nkb, body, jnp.zeros((CNT_ROWS, tq), jnp.int32))
        return jnp.sum(cnt, axis=0, keepdims=True)

    open0 = jnp.logical_and(q_adm_end > topk, q_idx < n_q_real).astype(jnp.int32)

    zero = jnp.zeros((1, tq), F32)
    c_nonneg = count(zero)
    c_pos = count(zero, strict=True)
    is_open0 = open0 > 0
    at_zero = jnp.logical_and(is_open0, jnp.logical_or(
        c_nonneg == topk, jnp.logical_and(c_pos < topk, c_nonneg > topk)))
    tie0 = jnp.logical_and(at_zero, c_nonneg > topk).astype(jnp.int32)
    thr0 = jnp.where(at_zero, 0.0, -FLT_MAX)
    open_a = jnp.where(at_zero, 0, open0)
    above_zero = c_pos >= topk
    lo0 = jnp.where(above_zero, 0.0, smin)
    hi0 = jnp.where(above_zero, smax * (1.0 + 2.0 ** -22) + FLT_TINY, 0.0)

    def search_cond(st):
        it, _, _, _, _, _, n_open = st
        return jnp.logical_and(it < SEARCH_PASS_CAP, n_open > 0)

    def search_body(st):
        it, lo, hi, thr, opn, tie, _ = st
        mid = 0.5 * lo + 0.5 * hi
        cnt = count(mid)
        is_open = opn > 0
        hit = jnp.logical_and(is_open, cnt == topk)
        stuck = jnp.logical_and(jnp.logical_and(is_open, cnt != topk),
                                jnp.logical_or(mid == lo, mid == hi))
        thr = jnp.where(hit, mid, jnp.where(stuck, lo, thr))
        tie = jnp.where(stuck, 1, tie)
        moving = jnp.logical_and(is_open, jnp.logical_not(jnp.logical_or(hit, stuck)))
        lo = jnp.where(jnp.logical_and(moving, cnt > topk), mid, lo)
        hi = jnp.where(jnp.logical_and(moving, cnt < topk), mid, hi)
        opn = jnp.where(jnp.logical_or(hit, stuck), 0, opn)
        return it + 1, lo, hi, thr, opn, tie, jnp.sum(opn)

    _, _, _, thr, _, tie, _ = lax.while_loop(
        search_cond, search_body, (jnp.int32(0), lo0, hi0, thr0, open_a, tie0, jnp.sum(open_a)))

    @pl.when(jnp.sum(tie) > 0)
    def _():
        cnt_gt = count(thr, strict=True)
        need = jnp.where(tie > 0, topk - cnt_gt, jnp.int32(2 ** 30)).astype(F32)
        r_i = lax.broadcasted_iota(jnp.int32, (sk, sk), 0)
        c_i = lax.broadcasted_iota(jnp.int32, (sk, sk), 1)
        tri = (c_i <= r_i).astype(BF16)

        def tie_body(j, seen):
            off = pl.multiple_of(j * sk, sk)
            blk = sc_scr[pl.ds(off, sk), :]
            eq = blk == thr
            rank = jnp.dot(tri, eq.astype(BF16), preferred_element_type=F32) + seen
            sc_scr[pl.ds(off, sk), :] = jnp.where(jnp.logical_and(eq, rank > need), -jnp.inf, blk)
            return rank[sk - 1:sk, :]

        lax.fori_loop(0, nkb, tie_body, jnp.zeros((1, tq), F32))

    grp = N_HEADS // N_KV_HEADS

    def finish():
        outs = []
        for gk in range(N_KV_HEADS):
            acc = acc_scr[gk]
            for hq in range(grp):
                sl = slice(hq * tq, (hq + 1) * tq)
                outs.append(acc[:HEAD_DIM, sl] / acc[HEAD_DIM:HEAD_DIM + 1, sl])
        o_ref[0] = jnp.concatenate(outs, axis=0).T.astype(o_ref.dtype)

    for gk in range(N_KV_HEADS):
        qb = qT_ref[0, 0, gk * HEAD_DIM:(gk + 1) * HEAD_DIM, :]
        qf = qb.astype(F32)
        qn2 = jnp.sum(qf * qf, axis=0, keepdims=True)
        m = jnp.sqrt(qn2 * kmax_ref[0, :, gk * HEAD_DIM:gk * HEAD_DIM + 1]) * M_SLACK
        row = lax.broadcasted_iota(jnp.int32, (BF16_SUBLANES, grp * tq), 0)
        qa_scr[gk, 0:HEAD_DIM, :] = qb
        qa_scr[gk, HEAD_DIM:HEAD_DIM + BF16_SUBLANES, :] = jnp.where(row == 0, -m, 0.0).astype(BF16)
        qa_scr[gk, HEAD_DIM + BF16_SUBLANES:, :] = jnp.zeros(
            (K_AUG - HEAD_DIM - BF16_SUBLANES, grp * tq), BF16)
    acc_scr[...] = jnp.zeros(acc_scr.shape, F32)

    def fast_body(j, carry):
        off = pl.multiple_of(j * sk, sk)
        sel = sc_scr[pl.ds(off, sk), :] >= thr
        logits = [jnp.dot(k_ref[0, pl.ds(off, sk), gk * K_AUG:(gk + 1) * K_AUG], qa_scr[gk],
                          preferred_element_type=F32) for gk in range(N_KV_HEADS)]
        for gk in range(N_KV_HEADS):
            s = logits[gk]
            p = jnp.concatenate(
                [jnp.exp2(jnp.where(sel, s[:, hq * tq:(hq + 1) * tq], NEG_BIG)) for hq in range(grp)],
                axis=1).astype(BF16)
            vtb = vT_ref[0, j, gk * V_ROWS:(gk + 1) * V_ROWS, :]
            acc_scr[gk] += jnp.dot(vtb, p, preferred_element_type=F32)
        return carry

    lax.fori_loop(0, nkb, fast_body, 0)
    l_min = jnp.min(acc_scr[:, HEAD_DIM:HEAD_DIM + 1, :])
    healthy = l_min > L_FLOOR

    @pl.when(healthy)
    def _():
        finish()

    @pl.when(jnp.logical_not(healthy))
    def _():
        m_scr[...] = jnp.full(m_scr.shape, NEG_BIG, F32)
        acc_scr[...] = jnp.zeros(acc_scr.shape, F32)

        def slow_body(j, carry):
            off = pl.multiple_of(j * sk, sk)
            sel = sc_scr[pl.ds(off, sk), :] >= thr
            for gk in range(N_KV_HEADS):
                kb = k_ref[0, pl.ds(off, sk), gk * K_AUG:gk * K_AUG + HEAD_DIM]
                vtb = vT_ref[0, j, gk * V_ROWS:(gk + 1) * V_ROWS, :]
                s = jnp.dot(kb, qT_ref[0, 0, gk * HEAD_DIM:(gk + 1) * HEAD_DIM, :],
                            preferred_element_type=F32)
                s = jnp.concatenate(
                    [jnp.where(sel, s[:, hq * tq:(hq + 1) * tq], NEG_BIG) for hq in range(grp)], axis=1)
                m_old = m_scr[gk]
                m_new = jnp.maximum(m_old, jnp.max(s, axis=0, keepdims=True))
                p = jnp.exp2(s - m_new).astype(BF16)
                alpha = jnp.exp2(m_old - m_new)
                acc_scr[gk] = alpha * acc_scr[gk] + jnp.dot(vtb, p, preferred_element_type=F32)
                m_scr[gk] = m_new
            return carry

        lax.fori_loop(0, nkb, slow_body, 0)
        finish()


def _kmax_kernel(k_ref, g_ref, o_ref):
    kf = k_ref[0].astype(F32)
    ss = jnp.dot((kf * kf).astype(BF16), g_ref[...], preferred_element_type=F32)
    o_ref[0] = jnp.max(ss, axis=0, keepdims=True)


def _kmax_call(k_b, g):
    bsz, s_pad, kw = k_b.shape
    return pl.pallas_call(
        _kmax_kernel,
        grid=(bsz,),
        in_specs=[pl.BlockSpec((1, s_pad, kw), lambda b: (b, 0, 0)), _const_spec(g.shape)],
        out_specs=pl.BlockSpec((1, 1, kw), lambda b: (b, 0, 0)),
        out_shape=jax.ShapeDtypeStruct((bsz, 1, kw), F32),
        compiler_params=_cparams(("arbitrary",)),
        name="kmax",
    )(k_b, g)


def _dsa_call(qg, qiT, wiT, k_aug, vT_blk, ki_b, kmax, *, tq, sk, s_real, q_pos0, n_q_real, topk):
    bsz, aw, tq_total = qiT.shape
    n_kblocks = vT_blk.shape[1]
    s_pad = k_aug.shape[1]
    nq = tq_total // tq
    grp = N_HEADS // N_KV_HEADS
    kern = functools.partial(_dsa_kernel, tq=tq, sk=sk, s_real=s_real, n_kblocks=n_kblocks,
                             q_pos0=q_pos0, n_q_real=n_q_real, topk=topk)
    return pl.pallas_call(
        kern,
        grid=(bsz, nq),
        in_specs=[
            pl.BlockSpec((1, 1, N_KV_HEADS * HEAD_DIM, grp * tq), lambda b, i: (b, i, 0, 0)),
            pl.BlockSpec((1, aw, tq), lambda b, i: (b, 0, i)),
            pl.BlockSpec((1, N_IDX_HEADS, tq), lambda b, i: (b, 0, i)),
            pl.BlockSpec((1, s_pad, N_KV_HEADS * K_AUG), lambda b, i: (b, 0, 0)),
            pl.BlockSpec((1, n_kblocks, N_KV_HEADS * V_ROWS, sk), lambda b, i: (b, 0, 0, 0)),
            pl.BlockSpec((1, s_pad, IDX_DIM), lambda b, i: (b, 0, 0)),
            pl.BlockSpec((1, 1, N_KV_HEADS * HEAD_DIM), lambda b, i: (b, 0, 0)),
        ],
        out_specs=pl.BlockSpec((1, tq, aw), lambda b, i: (b, i, 0)),
        out_shape=jax.ShapeDtypeStruct((bsz, tq_total, aw), BF16),
        scratch_shapes=[
            pltpu.VMEM((s_pad, tq), F32),
            pltpu.VMEM((N_KV_HEADS, 1, grp * tq), F32),
            pltpu.VMEM((N_KV_HEADS, V_ROWS, grp * tq), F32),
            pltpu.VMEM((N_KV_HEADS, K_AUG, grp * tq), BF16),
        ],
        compiler_params=_cparams(("arbitrary", "arbitrary")),
        name="dsa",
    )(qg, qiT, wiT, k_aug, vT_blk, ki_b, kmax)


def _ffn_kernel(x_ref, ya_ref, yb_ref, woa_ref, wob_ref, nf_ref, wg_ref, wu_ref, wo_ref, o_ref, *, n_chunks):
    x1 = x_ref[...] + jnp.dot(ya_ref[...], woa_ref[...], preferred_element_type=F32)
    x1 = x1 + jnp.dot(yb_ref[...], wob_ref[...], preferred_element_type=F32)
    ms = jnp.mean(x1 * x1, axis=-1, keepdims=True)
    hf = (x1 * lax.rsqrt(ms + EPS) * nf_ref[...]).astype(BF16)

    def body(c, acc):
        gt = jnp.dot(hf, wg_ref[c], preferred_element_type=F32)
        up = jnp.dot(hf, wu_ref[c], preferred_element_type=F32)
        act = (gt * jax.nn.sigmoid(gt) * up).astype(BF16)
        return acc + jnp.dot(act, wo_ref[c], preferred_element_type=F32)

    o_ref[...] = lax.fori_loop(0, n_chunks, body, x1)


def _ffn_call(x2, ya2, yb2, wts, *, tm):
    n, d = x2.shape
    w = ya2.shape[1]
    aw = yb2.shape[1]
    n_chunks, _, fc = wts["w_g"].shape
    single = pl.Buffered(1)

    def wspec(shape):
        nd = len(shape)
        return pl.BlockSpec(shape, lambda *_: (0,) * nd, pipeline_mode=single)

    return pl.pallas_call(
        functools.partial(_ffn_kernel, n_chunks=n_chunks),
        grid=(n // tm,),
        in_specs=[
            pl.BlockSpec((tm, d), lambda r: (r, 0)),
            pl.BlockSpec((tm, w), lambda r: (r, 0)),
            pl.BlockSpec((tm, aw), lambda r: (r, 0)),
            wspec(wts["w_out_a"].shape), wspec(wts["w_out_b"].shape), wspec(wts["norm_ffn"].shape),
            wspec(wts["w_g"].shape), wspec(wts["w_u"].shape), wspec(wts["w_o"].shape),
        ],
        out_specs=pl.BlockSpec((tm, d), lambda r: (r, 0)),
        out_shape=jax.ShapeDtypeStruct((n, d), F32),
        compiler_params=_cparams(("arbitrary",)),
        name="ffn",
    )(x2, ya2, yb2, wts["w_out_a"], wts["w_out_b"], wts["norm_ffn"], wts["w_g"], wts["w_u"], wts["w_o"])


def _block_diag(wb):
    n, a, b = wb.shape
    eye = jnp.eye(n, dtype=wb.dtype)
    return (eye[:, None, :, None] * wb[:, :, None, :]).reshape(n * a, n * b)


def _ffn_chunk(d_ff):
    for c in (512, 256, 128):
        if d_ff % c == 0:
            return c
    return d_ff


def _prep_weights(norm_mix, w_in, conv_w, conv_b, w_rg, b_rg, w_ig, b_ig, lru_lambda,
                  q_norm, k_norm, w_out, norm_ffn, w_ffn_in, w_ffn_out):
    d = w_in.shape[0]
    w = conv_b.shape[0]
    aw = N_HEADS * HEAD_DIM
    kw = N_KV_HEADS * HEAD_DIM
    iw = N_IDX_HEADS * IDX_DIM
    o = np.cumsum([0, w, w, aw, kw, kw, iw, IDX_DIM, N_IDX_HEADS])
    wb = w_in.astype(BF16)
    w_kw = jnp.pad(wb[:, o[6]:o[8]], ((0, 0), (0, LANES - (IDX_DIM + N_IDX_HEADS))))
    d_ff = w_ffn_out.shape[0]
    fc = _ffn_chunk(d_ff)
    nck = d_ff // fc
    wfi = w_ffn_in.astype(BF16)
    hd_id = np.arange(aw) // HEAD_DIM
    return {
        "norm_mix": norm_mix.reshape(1, d),
        "w_a": wb[:, o[0]:o[2]],
        "w_q": wb[:, o[2]:o[3]],
        "w_kv": wb[:, o[3]:o[5]],
        "w_qi": wb[:, o[5]:o[6]],
        "w_kw": w_kw,
        "conv_w": jnp.pad(conv_w, ((0, SUBLANES - CONV_W), (0, 0))),
        "conv_b": conv_b.reshape(1, w),
        "w_rg": _block_diag(w_rg).astype(BF16),
        "w_ig": _block_diag(w_ig).astype(BF16),
        "b_rg": b_rg.reshape(1, w),
        "b_ig": b_ig.reshape(1, w),
        "lam": lru_lambda.reshape(1, w),
        "q_norm": jnp.tile(q_norm, N_HEADS).reshape(1, aw),
        "k_norm": jnp.tile(k_norm, N_KV_HEADS).reshape(1, kw),
        "g": jnp.asarray(hd_id[:, None] == hd_id[None, :], BF16),
        "w_out_a": w_out[:w].astype(BF16),
        "w_out_b": w_out[w:].astype(BF16),
        "norm_ffn": norm_ffn.reshape(1, d),
        "w_g": wfi[:, :d_ff].reshape(d, nck, fc).transpose(1, 0, 2),
        "w_u": wfi[:, d_ff:].reshape(d, nck, fc).transpose(1, 0, 2),
        "w_o": w_ffn_out.astype(BF16).reshape(nck, fc, d),
    }


def _rope_tables(pos):
    half = HEAD_DIM // 2
    inv = ROPE_THETA ** (-jnp.arange(half, dtype=F32) / half)
    ang = pos.astype(F32)[:, None] * inv[None, :]
    cos = jnp.cos(ang)
    sin = jnp.sin(ang)
    reps = LANES // HEAD_DIM
    return (jnp.tile(jnp.concatenate([cos, cos], axis=1), (1, reps)),
            jnp.tile(jnp.concatenate([-sin, sin], axis=1), (1, reps)))


def _round_up(n, m):
    return (n + m - 1) // m * m


def _layer(x, pos0, conv_prev, h_prev, k_past, v_past, ki_past, wts):
    bsz, t_len, d = x.shape
    w = conv_prev.shape[2]
    kw = N_KV_HEADS * HEAD_DIM
    pos = pos0 + jnp.arange(t_len, dtype=jnp.int32)
    cos, sin = _rope_tables(pos)
    tb = 256 if t_len % 256 == 0 else t_len
    cprev = jnp.pad(conv_prev, ((0, 0), (SUBLANES - (CONV_W - 1), 0), (0, 0)))
    ya, q, qi, kv, kwi, tail, hlast = _proj_call(
        x, cos, sin, cprev, h_prev.reshape(bsz, 1, w), wts, tb=tb, pos0=pos0)
    k_new = kv[..., :kw]
    v_new = kv[..., kw:]
    ki_new = kwi[..., :IDX_DIM]
    wi = kwi[..., IDX_DIM:IDX_DIM + N_IDX_HEADS]

    if k_past is None:
        k_all, v_all, ki_all = k_new, v_new, ki_new
    else:
        p_len = k_past.shape[1]
        k_all = jnp.concatenate([k_past.reshape(bsz, p_len, kw), k_new], axis=1)
        v_all = jnp.concatenate([v_past.reshape(bsz, p_len, kw), v_new], axis=1)
        ki_all = jnp.concatenate([ki_past, ki_new], axis=1)
    s_real = k_all.shape[1]
    topk = min(TOPK_MAX, s_real // 4)

    tq = 256 if t_len % 256 == 0 else LANES
    tq_total = _round_up(t_len, tq)
    sk = 512
    s_pad = _round_up(s_real, sk)
    nkb = s_pad // sk
    pad_q = ((0, 0), (0, tq_total - t_len), (0, 0))
    pad_s = ((0, 0), (0, s_pad - s_real), (0, 0))
    grp = N_HEADS // N_KV_HEADS
    qg = jnp.pad(q, pad_q).reshape(bsz, tq_total // tq, tq, N_KV_HEADS, grp, HEAD_DIM)
    qg = qg.transpose(0, 1, 3, 5, 4, 2).reshape(bsz, tq_total // tq, N_KV_HEADS * HEAD_DIM, grp * tq)
    qiT = jnp.pad(qi, pad_q).transpose(0, 2, 1)
    wiT = jnp.pad(wi, pad_q).transpose(0, 2, 1)
    k_b = jnp.pad(k_all.astype(BF16), pad_s)
    ki_b = jnp.pad(ki_all.astype(BF16), pad_s)
    v_b = jnp.pad(v_all.astype(BF16), pad_s).reshape(bsz, nkb, sk, N_KV_HEADS, HEAD_DIM)
    vT = v_b.transpose(0, 1, 3, 4, 2)
    ones = jnp.ones((bsz, nkb, N_KV_HEADS, 1, sk), BF16)
    zeros = jnp.zeros((bsz, nkb, N_KV_HEADS, V_ROWS - HEAD_DIM - 1, sk), BF16)
    vT_blk = jnp.concatenate([vT, ones, zeros], axis=3).reshape(bsz, nkb, N_KV_HEADS * V_ROWS, sk)
    kmax = _kmax_call(k_b, wts["g"][:kw, :kw])
    k4 = k_b.reshape(bsz, s_pad, N_KV_HEADS, HEAD_DIM)
    k_aug = jnp.concatenate(
        [k4, jnp.ones((bsz, s_pad, N_KV_HEADS, 1), BF16),
         jnp.zeros((bsz, s_pad, N_KV_HEADS, K_AUG - HEAD_DIM - 1), BF16)], axis=3,
    ).reshape(bsz, s_pad, N_KV_HEADS * K_AUG)
    yb = _dsa_call(qg, qiT, wiT, k_aug, vT_blk, ki_b, kmax, tq=tq, sk=sk, s_real=s_real,
                   q_pos0=pos0, n_q_real=t_len, topk=topk)[:, :t_len]

    n = bsz * t_len
    tm = 512 if n % 512 == 0 else n
    y = _ffn_call(x.reshape(n, d), ya.reshape(n, w), yb.reshape(n, N_HEADS * HEAD_DIM), wts, tm=tm)
    return (y.reshape(bsz, t_len, d),
            k_new.reshape(bsz, t_len, N_KV_HEADS, HEAD_DIM),
            v_new.reshape(bsz, t_len, N_KV_HEADS, HEAD_DIM),
            ki_new,
            hlast[:, SUBLANES - 1],
            tail[:, SUBLANES - (CONV_W - 1):])


def kernel(x_prompt, x_sample, cache_k, cache_v, cache_kidx, state_h, state_conv, norm_mix, w_in, conv_w, conv_b, w_rg, b_rg, w_ig, b_ig, lru_lambda, q_norm, k_norm, w_out, norm_ffn, w_ffn_in, w_ffn_out):
    wts = _prep_weights(norm_mix, w_in, conv_w, conv_b, w_rg, b_rg, w_ig, b_ig, lru_lambda,
                        q_norm, k_norm, w_out, norm_ffn, w_ffn_in, w_ffn_out)
    bp = x_prompt.shape[0]
    w = conv_b.shape[0]
    p_len = cache_k.shape[1]
    conv0 = jnp.zeros((bp, CONV_W - 1, w), x_prompt.dtype)
    h0 = jnp.zeros((bp, w), x_prompt.dtype)
    yp, k_p, v_p, ki_p, h_p, conv_p = _layer(x_prompt, 0, conv0, h0, None, None, None, wts)
    ys, k_s, v_s, ki_s, h_s, conv_s = _layer(x_sample, p_len, state_conv, state_h,
                                             cache_k, cache_v, cache_kidx, wts)
    return (yp, ys, k_p, v_p, ki_p, h_p, conv_p, k_s, v_s, ki_s, h_s, conv_s)
```

```python
import functools

import numpy as np
import jax
import jax.numpy as jnp
from jax import lax
from jax.experimental import pallas as pl
from jax.experimental.pallas import tpu as pltpu

F32 = jnp.float32
BF16 = jnp.bfloat16

CHUNK = 64
EPS = 1e-6
LRU_BLOCKS = 8
CONV_W = 4
LRU_C = 8.0
N_HEADS = 8
N_KV_HEADS = 2
HEAD_DIM = 64
N_IDX_HEADS = 8
IDX_DIM = 64
TOPK_MAX = 256
ROPE_THETA = 10000.0

LANES = 128
SUBLANES = 8
VMEM_LIMIT = 48 * 1024 * 1024
FLT_MAX = float(np.finfo(np.float32).max)
FLT_TINY = float(np.finfo(np.float32).tiny)
SEARCH_PASS_CAP = 512
NEG_BIG = -1e30
LOG2_E = float(np.log2(np.e))
K_AUG = LANES
M_SLACK = 1.0 + 2.0 ** -5
CNT_ROWS = 4 * SUBLANES
L_FLOOR = 2.0 ** -80
BF16_SUBLANES = 16
PROJ_ROWS = 256
DSA_KEY_BLOCK = 512
V_ROWS = HEAD_DIM + BF16_SUBLANES


def _cparams(sem):
    return pltpu.CompilerParams(dimension_semantics=sem, vmem_limit_bytes=VMEM_LIMIT)


def _const_spec(shape):
    nd = len(shape)
    return pl.BlockSpec(shape, lambda *_: (0,) * nd)


def _swap_halves(z):
    lane = lax.broadcasted_iota(jnp.int32, z.shape, 1)
    lo_half = (lane % HEAD_DIM) < (HEAD_DIM // 2)
    return jnp.where(lo_half, pltpu.roll(z, LANES - HEAD_DIM // 2, 1), pltpu.roll(z, HEAD_DIM // 2, 1))


def _rope(z, cos, sin):
    outs = []
    for s in range(z.shape[1] // LANES):
        zs = z[:, s * LANES:(s + 1) * LANES]
        outs.append(zs * cos + _swap_halves(zs) * sin)
    return outs[0] if len(outs) == 1 else jnp.concatenate(outs, axis=1)


def _head_rmsnorm(z, gain, g):
    z2 = z * z
    hi = z2.astype(BF16)
    lo = (z2 - hi.astype(F32)).astype(BF16)
    ss = jnp.dot(hi, g, preferred_element_type=F32) + jnp.dot(lo, g, preferred_element_type=F32)
    return z * lax.rsqrt(ss * (1.0 / HEAD_DIM) + EPS) * gain


def _gelu_tanh(x):
    return 0.5 * x * (1.0 + jnp.tanh(np.sqrt(2.0 / np.pi).astype(np.float32) * (x + 0.044715 * (x * x * x))))


def _proj_kernel(x_ref, cos_ref, sin_ref, cprev_ref, hprev_ref,
                 nmix_ref, wa_ref, wq_ref, wkv_ref, wqi_ref, wkw_ref,
                 convw_ref, convb_ref, wrg_ref, wig_ref, brg_ref, big_ref, lam_ref,
                 qn_ref, kn_ref, g_ref,
                 ya_ref, kv_ref, kw_ref, tail_ref, hlast_ref, *rest, tb, pos0, wi_scale, dsa_layout):
    cbuf, hcar = rest[-2:]
    t = pl.program_id(1)
    w = cbuf.shape[1]

    @pl.when(t == 0)
    def _():
        cbuf[0:SUBLANES, :] = cprev_ref[0]
        hcar[...] = hprev_ref[0]

    x = x_ref[0]
    ms = jnp.mean(x * x, axis=-1, keepdims=True)
    h = (x * lax.rsqrt(ms + EPS) * nmix_ref[...]).astype(BF16)

    pa = jnp.dot(h, wa_ref[...], preferred_element_type=F32)
    xr = pa[:, :w]
    gate = pa[:, w:]
    cbuf[SUBLANES:SUBLANES + tb, :] = xr
    xc = convb_ref[...]
    for j in range(CONV_W):
        off = SUBLANES - (CONV_W - 1) + j
        xc = xc + cbuf[off:off + tb, :] * convw_ref[j:j + 1, :]
    tail = cbuf[tb:tb + SUBLANES, :]
    cbuf[0:SUBLANES, :] = tail
    tail_ref[0] = tail

    xcb = xc.astype(BF16)
    r = jax.nn.sigmoid(jnp.dot(xcb, wrg_ref[...], preferred_element_type=F32) + brg_ref[...])
    ig = jax.nn.sigmoid(jnp.dot(xcb, wig_ref[...], preferred_element_type=F32) + big_ref[...])
    nl = -lam_ref[...]
    softplus = jnp.maximum(nl, 0.0) + jnp.log(1.0 + jnp.exp(-jnp.abs(nl)))
    log_a = (-LRU_C) * r * softplus
    a = jnp.exp(log_a)
    mult = jnp.sqrt(1.0 - jnp.exp(2.0 * log_a))
    row = lax.broadcasted_iota(jnp.int32, (tb, 1), 0)
    mult = jnp.where(row + (pos0 + t * tb) == 0, 1.0, mult)
    b = mult * ig * xc
    s = 1
    while s < tb:
        keep = row >= s
        a_s = jnp.where(keep, pltpu.roll(a, s, 0), 1.0)
        b_s = jnp.where(keep, pltpu.roll(b, s, 0), 0.0)
        b = a * b_s + b
        a = a * a_s
        s *= 2
    hs = a * hcar[...] + b
    hcar[...] = hs[tb - 1:tb, :]
    hlast_ref[0] = hs[tb - SUBLANES:tb, :]
    ya_ref[0] = (hs * _gelu_tanh(gate)).astype(ya_ref.dtype)

    cos = cos_ref[...]
    sin = sin_ref[...]
    g = g_ref[...]
    pq = jnp.dot(h, wq_ref[...], preferred_element_type=F32)
    q = _rope(_head_rmsnorm(pq, qn_ref[...], g), cos, sin) * (LOG2_E * HEAD_DIM ** -0.5)

    pkv = jnp.dot(h, wkv_ref[...], preferred_element_type=F32)
    kw_ = N_KV_HEADS * HEAD_DIM
    k = _rope(_head_rmsnorm(pkv[:, :kw_], kn_ref[...], g[:kw_, :kw_]), cos, sin)
    v = pkv[:, kw_:]
    kv_ref[0] = jnp.concatenate([k, v], axis=1)

    pqi = jnp.dot(h, wqi_ref[...], preferred_element_type=F32)
    qi = _rope(pqi, cos, sin)

    pkw = jnp.dot(h, wkw_ref[...], preferred_element_type=F32)
    lane = lax.broadcasted_iota(jnp.int32, pkw.shape, 1)
    kwv = jnp.where(lane < IDX_DIM, _rope(pkw, cos, sin), pkw * wi_scale)
    kw_ref[0] = kwv

    if not dsa_layout:
        q_ref, qi_ref = rest[:2]
        q_ref[0] = q.astype(q_ref.dtype)
        qi_ref[0] = qi.astype(qi_ref.dtype)
        return

    qg_ref, qiT_ref, wiT_ref, kaug_ref, kib_ref, vT_ref = rest[:6]
    grp = N_HEADS // N_KV_HEADS
    qT = q.T.astype(qg_ref.dtype)
    qg_ref[0, 0] = jnp.concatenate(
        [jnp.concatenate([qT[(gk * grp + hq) * HEAD_DIM:(gk * grp + hq + 1) * HEAD_DIM, :]
                          for hq in range(grp)], axis=1) for gk in range(N_KV_HEADS)], axis=0)
    qiT_ref[0] = qi.T.astype(qiT_ref.dtype)
    wiT_ref[0] = kwv.T[IDX_DIM:IDX_DIM + N_IDX_HEADS, :]
    kib_ref[0] = kwv[:, :IDX_DIM].astype(kib_ref.dtype)
    one_lane = jnp.where(lax.broadcasted_iota(jnp.int32, (tb, K_AUG - HEAD_DIM), 1) == 0, 1.0, 0.0)
    kaug_ref[0] = jnp.concatenate(
        [piece for gk in range(N_KV_HEADS)
         for piece in (k[:, gk * HEAD_DIM:(gk + 1) * HEAD_DIM], one_lane)], axis=1).astype(kaug_ref.dtype)
    vT = v.T
    one_row = jnp.where(lax.broadcasted_iota(jnp.int32, (V_ROWS - HEAD_DIM, tb), 0) == 0, 1.0, 0.0)
    vT_ref[0, 0] = jnp.concatenate(
        [piece for gk in range(N_KV_HEADS)
         for piece in (vT[gk * HEAD_DIM:(gk + 1) * HEAD_DIM, :], one_row)], axis=0).astype(vT_ref.dtype)


def _proj_call(x, cos, sin, cprev, hprev, wts, *, tb, pos0, sk=None):
    bsz, t_len, d = x.shape
    w = wts["conv_b"].shape[1]
    aw = N_HEADS * HEAD_DIM
    kvw = 2 * N_KV_HEADS * HEAD_DIM
    nt = t_len // tb
    wi_scale = float(N_IDX_HEADS ** -0.5 * IDX_DIM ** -0.5)
    names = ("norm_mix", "w_a", "w_q", "w_kv", "w_qi", "w_kw", "conv_w", "conv_b",
             "w_rg", "w_ig", "b_rg", "b_ig", "lam", "q_norm", "k_norm", "g")
    warrs = [wts[n] for n in names]
    in_specs = [
        pl.BlockSpec((1, tb, d), lambda b, t: (b, t, 0)),
        pl.BlockSpec((tb, LANES), lambda b, t: (t, 0)),
        pl.BlockSpec((tb, LANES), lambda b, t: (t, 0)),
        pl.BlockSpec((1, SUBLANES, w), lambda b, t: (b, 0, 0)),
        pl.BlockSpec((1, 1, w), lambda b, t: (b, 0, 0)),
    ] + [_const_spec(a.shape) for a in warrs]
    out_shape = [
        jax.ShapeDtypeStruct((bsz, t_len, w), BF16),
        jax.ShapeDtypeStruct((bsz, t_len, kvw), F32),
        jax.ShapeDtypeStruct((bsz, t_len, LANES), F32),
        jax.ShapeDtypeStruct((bsz, SUBLANES, w), F32),
        jax.ShapeDtypeStruct((bsz, SUBLANES, w), F32),
    ]
    out_specs = [
        pl.BlockSpec((1, tb, w), lambda b, t: (b, t, 0)),
        pl.BlockSpec((1, tb, kvw), lambda b, t: (b, t, 0)),
        pl.BlockSpec((1, tb, LANES), lambda b, t: (b, t, 0)),
        pl.BlockSpec((1, SUBLANES, w), lambda b, t: (b, 0, 0)),
        pl.BlockSpec((1, SUBLANES, w), lambda b, t: (b, 0, 0)),
    ]
    dsa_layout = sk is not None
    if dsa_layout:
        grp = N_HEADS // N_KV_HEADS
        per_sk = sk // tb
        out_shape += [
            jax.ShapeDtypeStruct((bsz, nt, kvw // 2, grp * tb), BF16),
            jax.ShapeDtypeStruct((bsz, aw, t_len), BF16),
            jax.ShapeDtypeStruct((bsz, N_IDX_HEADS, t_len), F32),
            jax.ShapeDtypeStruct((bsz, t_len, N_KV_HEADS * K_AUG), BF16),
            jax.ShapeDtypeStruct((bsz, t_len, IDX_DIM), BF16),
            jax.ShapeDtypeStruct((bsz, t_len // sk, N_KV_HEADS * V_ROWS, sk), BF16),
        ]
        out_specs += [
            pl.BlockSpec((1, 1, kvw // 2, grp * tb), lambda b, t: (b, t, 0, 0)),
            pl.BlockSpec((1, aw, tb), lambda b, t: (b, 0, t)),
            pl.BlockSpec((1, N_IDX_HEADS, tb), lambda b, t: (b, 0, t)),
            pl.BlockSpec((1, tb, N_KV_HEADS * K_AUG), lambda b, t: (b, t, 0)),
            pl.BlockSpec((1, tb, IDX_DIM), lambda b, t: (b, t, 0)),
            pl.BlockSpec((1, 1, N_KV_HEADS * V_ROWS, tb), lambda b, t: (b, t // per_sk, 0, t % per_sk)),
        ]
    else:
        out_shape += [jax.ShapeDtypeStruct((bsz, t_len, aw), BF16),
                      jax.ShapeDtypeStruct((bsz, t_len, aw), BF16)]
        out_specs += [pl.BlockSpec((1, tb, aw), lambda b, t: (b, t, 0)),
                      pl.BlockSpec((1, tb, aw), lambda b, t: (b, t, 0))]
    return pl.pallas_call(
        functools.partial(_proj_kernel, tb=tb, pos0=pos0, wi_scale=wi_scale, dsa_layout=dsa_layout),
        grid=(bsz, nt),
        in_specs=in_specs,
        out_specs=out_specs,
        out_shape=out_shape,
        scratch_shapes=[pltpu.VMEM((SUBLANES + tb, w), F32), pltpu.VMEM((1, w), F32)],
        compiler_params=_cparams(("arbitrary", "arbitrary")),
        name="proj",
    )(x, cos, sin, cprev, hprev, *warrs)


def _dsa_kernel(qT_ref, qiT_ref, wiT_ref, k_ref, vT_ref, ki_ref, kmax_ref, o_ref,
                sc_scr, m_scr, acc_scr, qa_scr, *, tq, sk, s_real, n_kblocks, q_pos0, n_q_real, topk):
    i = pl.program_id(1)
    q_idx = i * tq + lax.broadcasted_iota(jnp.int32, (1, tq), 1)
    q_adm_end = ((q_pos0 + q_idx) // CHUNK + 1) * CHUNK
    q_adm_end = jnp.minimum(q_adm_end, s_real)
    last_q = jnp.minimum(i * tq + tq - 1, n_q_real - 1)
    blk_end = jnp.minimum(((q_pos0 + last_q) // CHUNK + 1) * CHUNK, s_real)
    nkb = jnp.minimum((blk_end + sk - 1) // sk, n_kblocks)

    def score_body(j, carry):
        smax, smin = carry
        off = pl.multiple_of(j * sk, sk)
        kib = ki_ref[0, pl.ds(off, sk), :]
        acc = jnp.zeros((sk, tq), F32)
        for hh in range(N_IDX_HEADS):
            sc = jnp.dot(kib, qiT_ref[0, hh * IDX_DIM:(hh + 1) * IDX_DIM, :], preferred_element_type=F32)
            acc = acc + jnp.maximum(sc, 0.0) * wiT_ref[0, hh:hh + 1, :]
        kidx = off + lax.broadcasted_iota(jnp.int32, (sk, 1), 0)
        adm = kidx < q_adm_end
        hi_part = jnp.where(adm, acc, -jnp.inf)
        lo_part = jnp.where(adm, acc, jnp.inf)
        sc_scr[pl.ds(off, sk), :] = hi_part
        return (jnp.maximum(smax, jnp.max(hi_part.reshape(sk // CNT_ROWS, CNT_ROWS, tq), axis=0)),
                jnp.minimum(smin, jnp.min(lo_part.reshape(sk // CNT_ROWS, CNT_ROWS, tq), axis=0)))

    smax, smin = lax.fori_loop(
        0, nkb, score_body,
        (jnp.full((CNT_ROWS, tq), -jnp.inf, F32), jnp.full((CNT_ROWS, tq), jnp.inf, F32)))
    smax = jnp.max(smax, axis=0, keepdims=True)
    smin = jnp.min(smin, axis=0, keepdims=True)

    def count(thr, strict=False):
        def body(j, cnt):
            off = pl.multiple_of(j * sk, sk)
            blk = sc_scr[pl.ds(off, sk), :]
            above = (blk > thr if strict else blk >= thr).astype(jnp.int32)
            return cnt + jnp.sum(above.reshape(sk // CNT_ROWS, CNT_ROWS, tq), axis=0)
        cnt = lax.fori_loop(0, nkb, body, jnp.zeros((CNT_ROWS, tq), jnp.int32))
        return jnp.sum(cnt, axis=0, keepdims=True)

    open0 = jnp.logical_and(q_adm_end > topk, q_idx < n_q_real).astype(jnp.int32)

    zero = jnp.zeros((1, tq), F32)
    c_nonneg = count(zero)
    c_pos = count(zero, strict=True)
    is_open0 = open0 > 0
    at_zero = jnp.logical_and(is_open0, jnp.logical_or(
        c_nonneg == topk, jnp.logical_and(c_pos < topk, c_nonneg > topk)))
    tie0 = jnp.logical_and(at_zero, c_nonneg > topk).astype(jnp.int32)
    thr0 = jnp.where(at_zero, 0.0, -FLT_MAX)
    open_a = jnp.where(at_zero, 0, open0)
    above_zero = c_pos >= topk
    lo0 = jnp.where(above_zero, 0.0, smin)
    hi0 = jnp.where(above_zero, smax * (1.0 + 2.0 ** -22) + FLT_TINY, 0.0)

    def search_cond(st):
        it, _, _, _, _, _, n_open = st
        return jnp.logical_and(it < SEARCH_PASS_CAP, n_open > 0)

    def search_body(st):
        it, lo, hi, thr, opn, tie, _ = st
        mid = 0.5 * lo + 0.5 * hi
        cnt = count(mid)
        is_open = opn > 0
        hit = jnp.logical_and(is_open, cnt == topk)
        stuck = jnp.logical_and(jnp.logical_and(is_open, cnt != topk),
                                jnp.logical_or(mid == lo, mid == hi))
        thr = jnp.where(hit, mid, jnp.where(stuck, lo, thr))
        tie = jnp.where(stuck, 1, tie)
        moving = jnp.logical_and(is_open, jnp.logical_not(jnp.logical_or(hit, stuck)))
        lo = jnp.where(jnp.logical_and(moving, cnt > topk), mid, lo)
        hi = jnp.where(jnp.logical_and(moving, cnt < topk), mid, hi)
        opn = jnp.where(jnp.logical_or(hit, stuck), 0, opn)
        return it + 1, lo, hi, thr, opn, tie, jnp.sum(opn)

    _, _, _, thr, _, tie, _ = lax.while_loop(
        search_cond, search_body, (jnp.int32(0), lo0, hi0, thr0, open_a, tie0, jnp.sum(open_a)))

    @pl.when(jnp.sum(tie) > 0)
    def _():
        cnt_gt = count(thr, strict=True)
        need = jnp.where(tie > 0, topk - cnt_gt, jnp.int32(2 ** 30)).astype(F32)
        r_i = lax.broadcasted_iota(jnp.int32, (sk, sk), 0)
        c_i = lax.broadcasted_iota(jnp.int32, (sk, sk), 1)
        tri = (c_i <= r_i).astype(BF16)

        def tie_body(j, seen):
            off = pl.multiple_of(j * sk, sk)
            blk = sc_scr[pl.ds(off, sk), :]
            eq = blk == thr
            rank = jnp.dot(tri, eq.astype(BF16), preferred_element_type=F32) + seen
            sc_scr[pl.ds(off, sk), :] = jnp.where(jnp.logical_and(eq, rank > need), -jnp.inf, blk)
            return rank[sk - 1:sk, :]

        lax.fori_loop(0, nkb, tie_body, jnp.zeros((1, tq), F32))

    grp = N_HEADS // N_KV_HEADS

    def finish():
        outs = []
        for gk in range(N_KV_HEADS):
            acc = acc_scr[gk]
            for hq in range(grp):
                sl = slice(hq * tq, (hq + 1) * tq)
                outs.append(acc[:HEAD_DIM, sl] / acc[HEAD_DIM:HEAD_DIM + 1, sl])
        o_ref[0] = jnp.concatenate(outs, axis=0).T.astype(o_ref.dtype)

    for gk in range(N_KV_HEADS):
        qb = qT_ref[0, 0, gk * HEAD_DIM:(gk + 1) * HEAD_DIM, :]
        qf = qb.astype(F32)
        qn2 = jnp.sum(qf * qf, axis=0, keepdims=True)
        m = jnp.sqrt(qn2 * kmax_ref[0, :, gk * K_AUG:gk * K_AUG + 1]) * M_SLACK
        row = lax.broadcasted_iota(jnp.int32, (BF16_SUBLANES, grp * tq), 0)
        qa_scr[gk, 0:HEAD_DIM, :] = qb
        qa_scr[gk, HEAD_DIM:HEAD_DIM + BF16_SUBLANES, :] = jnp.where(row == 0, -m, 0.0).astype(BF16)
        qa_scr[gk, HEAD_DIM + BF16_SUBLANES:, :] = jnp.zeros(
            (K_AUG - HEAD_DIM - BF16_SUBLANES, grp * tq), BF16)
    acc_scr[...] = jnp.zeros(acc_scr.shape, F32)

    def fast_body(j, carry):
        off = pl.multiple_of(j * sk, sk)
        sel = sc_scr[pl.ds(off, sk), :] >= thr
        logits = [jnp.dot(k_ref[0, pl.ds(off, sk), gk * K_AUG:(gk + 1) * K_AUG], qa_scr[gk],
                          preferred_element_type=F32) for gk in range(N_KV_HEADS)]
        for gk in range(N_KV_HEADS):
            s = logits[gk]
            p = jnp.concatenate(
                [jnp.exp2(jnp.where(sel, s[:, hq * tq:(hq + 1) * tq], NEG_BIG)) for hq in range(grp)],
                axis=1).astype(BF16)
            vtb = vT_ref[0, j, gk * V_ROWS:(gk + 1) * V_ROWS, :]
            acc_scr[gk] += jnp.dot(vtb, p, preferred_element_type=F32)
        return carry

    lax.fori_loop(0, nkb, fast_body, 0)
    l_min = jnp.min(acc_scr[:, HEAD_DIM:HEAD_DIM + 1, :])
    healthy = l_min > L_FLOOR

    @pl.when(healthy)
    def _():
        finish()

    @pl.when(jnp.logical_not(healthy))
    def _():
        m_scr[...] = jnp.full(m_scr.shape, NEG_BIG, F32)
        acc_scr[...] = jnp.zeros(acc_scr.shape, F32)

        def slow_body(j, carry):
            off = pl.multiple_of(j * sk, sk)
            sel = sc_scr[pl.ds(off, sk), :] >= thr
            for gk in range(N_KV_HEADS):
                kb = k_ref[0, pl.ds(off, sk), gk * K_AUG:gk * K_AUG + HEAD_DIM]
                vtb = vT_ref[0, j, gk * V_ROWS:(gk + 1) * V_ROWS, :]
                s = jnp.dot(kb, qT_ref[0, 0, gk * HEAD_DIM:(gk + 1) * HEAD_DIM, :],
                            preferred_element_type=F32)
                s = jnp.concatenate(
                    [jnp.where(sel, s[:, hq * tq:(hq + 1) * tq], NEG_BIG) for hq in range(grp)], axis=1)
                m_old = m_scr[gk]
                m_new = jnp.maximum(m_old, jnp.max(s, axis=0, keepdims=True))
                p = jnp.exp2(s - m_new).astype(BF16)
                alpha = jnp.exp2(m_old - m_new)
                acc_scr[gk] = alpha * acc_scr[gk] + jnp.dot(vtb, p, preferred_element_type=F32)
                m_scr[gk] = m_new
            return carry

        lax.fori_loop(0, nkb, slow_body, 0)
        finish()


def _kmax_kernel(k_ref, g_ref, o_ref):
    kf = k_ref[0].astype(F32)
    ss = jnp.dot((kf * kf).astype(BF16), g_ref[...], preferred_element_type=F32)
    o_ref[0] = jnp.max(ss, axis=0, keepdims=True)


def _kmax_call(k_b, g):
    bsz, s_pad, kw = k_b.shape
    return pl.pallas_call(
        _kmax_kernel,
        grid=(bsz,),
        in_specs=[pl.BlockSpec((1, s_pad, kw), lambda b: (b, 0, 0)), _const_spec(g.shape)],
        out_specs=pl.BlockSpec((1, 1, kw), lambda b: (b, 0, 0)),
        out_shape=jax.ShapeDtypeStruct((bsz, 1, kw), F32),
        compiler_params=_cparams(("arbitrary",)),
        name="kmax",
    )(k_b, g)


def _dsa_call(qg, qiT, wiT, k_aug, vT_blk, ki_b, kmax, *, tq, sk, s_real, q_pos0, n_q_real, topk):
    bsz, aw, tq_total = qiT.shape
    n_kblocks = vT_blk.shape[1]
    s_pad = k_aug.shape[1]
    nq = tq_total // tq
    grp = N_HEADS // N_KV_HEADS
    kern = functools.partial(_dsa_kernel, tq=tq, sk=sk, s_real=s_real, n_kblocks=n_kblocks,
                             q_pos0=q_pos0, n_q_real=n_q_real, topk=topk)
    return pl.pallas_call(
        kern,
        grid=(bsz, nq),
        in_specs=[
            pl.BlockSpec((1, 1, N_KV_HEADS * HEAD_DIM, grp * tq), lambda b, i: (b, i, 0, 0)),
            pl.BlockSpec((1, aw, tq), lambda b, i: (b, 0, i)),
            pl.BlockSpec((1, N_IDX_HEADS, tq), lambda b, i: (b, 0, i)),
            pl.BlockSpec((1, s_pad, N_KV_HEADS * K_AUG), lambda b, i: (b, 0, 0)),
            pl.BlockSpec((1, n_kblocks, N_KV_HEADS * V_ROWS, sk), lambda b, i: (b, 0, 0, 0)),
            pl.BlockSpec((1, s_pad, IDX_DIM), lambda b, i: (b, 0, 0)),
            pl.BlockSpec((1, 1, N_KV_HEADS * K_AUG), lambda b, i: (b, 0, 0)),
        ],
        out_specs=pl.BlockSpec((1, tq, aw), lambda b, i: (b, i, 0)),
        out_shape=jax.ShapeDtypeStruct((bsz, tq_total, aw), BF16),
        scratch_shapes=[
            pltpu.VMEM((s_pad, tq), F32),
            pltpu.VMEM((N_KV_HEADS, 1, grp * tq), F32),
            pltpu.VMEM((N_KV_HEADS, V_ROWS, grp * tq), F32),
            pltpu.VMEM((N_KV_HEADS, K_AUG, grp * tq), BF16),
        ],
        compiler_params=_cparams(("arbitrary", "arbitrary")),
        name="dsa",
    )(qg, qiT, wiT, k_aug, vT_blk, ki_b, kmax)


def _ffn_kernel(x_ref, ya_ref, yb_ref, woa_ref, wob_ref, nf_ref, wg_ref, wu_ref, wo_ref, o_ref, *, n_chunks):
    x1 = x_ref[...] + jnp.dot(ya_ref[...], woa_ref[...], preferred_element_type=F32)
    x1 = x1 + jnp.dot(yb_ref[...], wob_ref[...], preferred_element_type=F32)
    ms = jnp.mean(x1 * x1, axis=-1, keepdims=True)
    hf = (x1 * lax.rsqrt(ms + EPS) * nf_ref[...]).astype(BF16)

    def body(c, acc):
        gt = jnp.dot(hf, wg_ref[c], preferred_element_type=F32)
        up = jnp.dot(hf, wu_ref[c], preferred_element_type=F32)
        act = (gt * jax.nn.sigmoid(gt) * up).astype(BF16)
        return acc + jnp.dot(act, wo_ref[c], preferred_element_type=F32)

    o_ref[...] = lax.fori_loop(0, n_chunks, body, x1)


def _ffn_call(x2, ya2, yb2, wts, *, tm):
    n, d = x2.shape
    w = ya2.shape[1]
    aw = yb2.shape[1]
    n_chunks, _, fc = wts["w_g"].shape
    single = pl.Buffered(1)

    def wspec(shape):
        nd = len(shape)
        return pl.BlockSpec(shape, lambda *_: (0,) * nd, pipeline_mode=single)

    return pl.pallas_call(
        functools.partial(_ffn_kernel, n_chunks=n_chunks),
        grid=(n // tm,),
        in_specs=[
            pl.BlockSpec((tm, d), lambda r: (r, 0)),
            pl.BlockSpec((tm, w), lambda r: (r, 0)),
            pl.BlockSpec((tm, aw), lambda r: (r, 0)),
            wspec(wts["w_out_a"].shape), wspec(wts["w_out_b"].shape), wspec(wts["norm_ffn"].shape),
            wspec(wts["w_g"].shape), wspec(wts["w_u"].shape), wspec(wts["w_o"].shape),
        ],
        out_specs=pl.BlockSpec((tm, d), lambda r: (r, 0)),
        out_shape=jax.ShapeDtypeStruct((n, d), F32),
        compiler_params=_cparams(("arbitrary",)),
        name="ffn",
    )(x2, ya2, yb2, wts["w_out_a"], wts["w_out_b"], wts["norm_ffn"], wts["w_g"], wts["w_u"], wts["w_o"])


def _block_diag(wb):
    n, a, b = wb.shape
    eye = jnp.eye(n, dtype=wb.dtype)
    return (eye[:, None, :, None] * wb[:, :, None, :]).reshape(n * a, n * b)


def _ffn_chunk(d_ff):
    for c in (d_ff, 512, 256, 128):
        if d_ff % c == 0:
            return c
    return d_ff


def _prep_weights(norm_mix, w_in, conv_w, conv_b, w_rg, b_rg, w_ig, b_ig, lru_lambda,
                  q_norm, k_norm, w_out, norm_ffn, w_ffn_in, w_ffn_out):
    d = w_in.shape[0]
    w = conv_b.shape[0]
    aw = N_HEADS * HEAD_DIM
    kw = N_KV_HEADS * HEAD_DIM
    iw = N_IDX_HEADS * IDX_DIM
    o = np.cumsum([0, w, w, aw, kw, kw, iw, IDX_DIM, N_IDX_HEADS])
    wb = w_in.astype(BF16)
    w_kw = jnp.pad(wb[:, o[6]:o[8]], ((0, 0), (0, LANES - (IDX_DIM + N_IDX_HEADS))))
    d_ff = w_ffn_out.shape[0]
    fc = _ffn_chunk(d_ff)
    nck = d_ff // fc
    wfi = w_ffn_in.astype(BF16)
    hd_id = np.arange(aw) // HEAD_DIM
    aug_id = np.arange(N_KV_HEADS * K_AUG)
    return {
        "norm_mix": norm_mix.reshape(1, d),
        "w_a": wb[:, o[0]:o[2]],
        "w_q": wb[:, o[2]:o[3]],
        "w_kv": wb[:, o[3]:o[5]],
        "w_qi": wb[:, o[5]:o[6]],
        "w_kw": w_kw,
        "conv_w": jnp.pad(conv_w, ((0, SUBLANES - CONV_W), (0, 0))),
        "conv_b": conv_b.reshape(1, w),
        "w_rg": _block_diag(w_rg).astype(BF16),
        "w_ig": _block_diag(w_ig).astype(BF16),
        "b_rg": b_rg.reshape(1, w),
        "b_ig": b_ig.reshape(1, w),
        "lam": lru_lambda.reshape(1, w),
        "q_norm": jnp.tile(q_norm, N_HEADS).reshape(1, aw),
        "k_norm": jnp.tile(k_norm, N_KV_HEADS).reshape(1, kw),
        "g": jnp.asarray(hd_id[:, None] == hd_id[None, :], BF16),
        "g_aug": jnp.asarray((aug_id[:, None] // K_AUG == aug_id[None, :] // K_AUG)
                             & (aug_id[:, None] % K_AUG < HEAD_DIM), BF16),
        "w_out_a": w_out[:w].astype(BF16),
        "w_out_b": w_out[w:].astype(BF16),
        "norm_ffn": norm_ffn.reshape(1, d),
        "w_g": wfi[:, :d_ff].reshape(d, nck, fc).transpose(1, 0, 2),
        "w_u": wfi[:, d_ff:].reshape(d, nck, fc).transpose(1, 0, 2),
        "w_o": w_ffn_out.astype(BF16).reshape(nck, fc, d),
    }


def _rope_tables(pos):
    half = HEAD_DIM // 2
    inv = ROPE_THETA ** (-jnp.arange(half, dtype=F32) / half)
    ang = pos.astype(F32)[:, None] * inv[None, :]
    cos = jnp.cos(ang)
    sin = jnp.sin(ang)
    reps = LANES // HEAD_DIM
    return (jnp.tile(jnp.concatenate([cos, cos], axis=1), (1, reps)),
            jnp.tile(jnp.concatenate([-sin, sin], axis=1), (1, reps)))


def _round_up(n, m):
    return (n + m - 1) // m * m


def _dsa_operands(q, qi, wi, k_all, v_all, ki_all, *, tq, sk):
    bsz, t_len, _ = q.shape
    s_real = k_all.shape[1]
    tq_total = _round_up(t_len, tq)
    s_pad = _round_up(s_real, sk)
    nkb = s_pad // sk
    pad_q = ((0, 0), (0, tq_total - t_len), (0, 0))
    pad_s = ((0, 0), (0, s_pad - s_real), (0, 0))
    grp = N_HEADS // N_KV_HEADS
    qg = jnp.pad(q, pad_q).reshape(bsz, tq_total // tq, tq, N_KV_HEADS, grp, HEAD_DIM)
    qg = qg.transpose(0, 1, 3, 5, 4, 2).reshape(bsz, tq_total // tq, N_KV_HEADS * HEAD_DIM, grp * tq)
    qiT = jnp.pad(qi, pad_q).transpose(0, 2, 1)
    wiT = jnp.pad(wi, pad_q).transpose(0, 2, 1)
    ki_b = jnp.pad(ki_all.astype(BF16), pad_s)
    v_b = jnp.pad(v_all.astype(BF16), pad_s).reshape(bsz, nkb, sk, N_KV_HEADS, HEAD_DIM)
    vT = v_b.transpose(0, 1, 3, 4, 2)
    ones = jnp.ones((bsz, nkb, N_KV_HEADS, 1, sk), BF16)
    zeros = jnp.zeros((bsz, nkb, N_KV_HEADS, V_ROWS - HEAD_DIM - 1, sk), BF16)
    vT_blk = jnp.concatenate([vT, ones, zeros], axis=3).reshape(bsz, nkb, N_KV_HEADS * V_ROWS, sk)
    k4 = jnp.pad(k_all.astype(BF16), pad_s).reshape(bsz, s_pad, N_KV_HEADS, HEAD_DIM)
    k_aug = jnp.concatenate(
        [k4, jnp.ones((bsz, s_pad, N_KV_HEADS, 1), BF16),
         jnp.zeros((bsz, s_pad, N_KV_HEADS, K_AUG - HEAD_DIM - 1), BF16)], axis=3,
    ).reshape(bsz, s_pad, N_KV_HEADS * K_AUG)
    return qg, qiT, wiT, k_aug, vT_blk, ki_b


def _layer(x, pos0, conv_prev, h_prev, k_past, v_past, ki_past, wts):
    bsz, t_len, d = x.shape
    w = conv_prev.shape[2]
    kw = N_KV_HEADS * HEAD_DIM
    pos = pos0 + jnp.arange(t_len, dtype=jnp.int32)
    cos, sin = _rope_tables(pos)
    tb = PROJ_ROWS if t_len % PROJ_ROWS == 0 else t_len
    sk = DSA_KEY_BLOCK
    fused = k_past is None and t_len % sk == 0 and tb == PROJ_ROWS
    tq = tb if fused else LANES
    cprev = jnp.pad(conv_prev, ((0, 0), (SUBLANES - (CONV_W - 1), 0), (0, 0)))
    outs = _proj_call(x, cos, sin, cprev, h_prev.reshape(bsz, 1, w), wts, tb=tb, pos0=pos0,
                      sk=sk if fused else None)
    ya, kv, kwi, tail, hlast = outs[:5]
    k_new = kv[..., :kw]
    v_new = kv[..., kw:]
    ki_new = kwi[..., :IDX_DIM]
    if fused:
        s_real = t_len
        qg, qiT, wiT, k_aug, ki_b, vT_blk = outs[5:]
    else:
        q, qi = outs[5:]
        wi = kwi[..., IDX_DIM:IDX_DIM + N_IDX_HEADS]
        if k_past is None:
            k_all, v_all, ki_all = k_new, v_new, ki_new
        else:
            p_len = k_past.shape[1]
            k_all = jnp.concatenate([k_past.reshape(bsz, p_len, kw), k_new], axis=1)
            v_all = jnp.concatenate([v_past.reshape(bsz, p_len, kw), v_new], axis=1)
            ki_all = jnp.concatenate([ki_past, ki_new], axis=1)
        s_real = k_all.shape[1]
        qg, qiT, wiT, k_aug, vT_blk, ki_b = _dsa_operands(q, qi, wi, k_all, v_all, ki_all, tq=tq, sk=sk)
    topk = min(TOPK_MAX, s_real // 4)
    kmax = _kmax_call(k_aug, wts["g_aug"])
    yb = _dsa_call(qg, qiT, wiT, k_aug, vT_blk, ki_b, kmax, tq=tq, sk=sk, s_real=s_real,
                   q_pos0=pos0, n_q_real=t_len, topk=topk)[:, :t_len]

    n = bsz * t_len
    tm = 512 if n % 512 == 0 else n
    y = _ffn_call(x.reshape(n, d), ya.reshape(n, w), yb.reshape(n, N_HEADS * HEAD_DIM), wts, tm=tm)
    return (y.reshape(bsz, t_len, d),
            k_new.reshape(bsz, t_len, N_KV_HEADS, HEAD_DIM),
            v_new.reshape(bsz, t_len, N_KV_HEADS, HEAD_DIM),
            ki_new,
            hlast[:, SUBLANES - 1],
            tail[:, SUBLANES - (CONV_W - 1):])


def kernel(x_prompt, x_sample, cache_k, cache_v, cache_kidx, state_h, state_conv, norm_mix, w_in, conv_w, conv_b, w_rg, b_rg, w_ig, b_ig, lru_lambda, q_norm, k_norm, w_out, norm_ffn, w_ffn_in, w_ffn_out):
    wts = _prep_weights(norm_mix, w_in, conv_w, conv_b, w_rg, b_rg, w_ig, b_ig, lru_lambda,
                        q_norm, k_norm, w_out, norm_ffn, w_ffn_in, w_ffn_out)
    bp = x_prompt.shape[0]
    w = conv_b.shape[0]
    p_len = cache_k.shape[1]
    conv0 = jnp.zeros((bp, CONV_W - 1, w), x_prompt.dtype)
    h0 = jnp.zeros((bp, w), x_prompt.dtype)
    yp, k_p, v_p, ki_p, h_p, conv_p = _layer(x_prompt, 0, conv0, h0, None, None, None, wts)
    ys, k_s, v_s, ki_s, h_s, conv_s = _layer(x_sample, p_len, state_conv, state_h,
                                             cache_k, cache_v, cache_kidx, wts)
    return (yp, ys, k_p, v_p, ki_p, h_p, conv_p, k_s, v_s, ki_s, h_s, conv_s)
```

```python
import functools

import numpy as np
import jax
import jax.numpy as jnp
from jax import lax
from jax.experimental import pallas as pl
from jax.experimental.pallas import tpu as pltpu

F32 = jnp.float32
BF16 = jnp.bfloat16

CHUNK = 64
EPS = 1e-6
LRU_BLOCKS = 8
CONV_W = 4
LRU_C = 8.0
N_HEADS = 8
N_KV_HEADS = 2
HEAD_DIM = 64
N_IDX_HEADS = 8
IDX_DIM = 64
TOPK_MAX = 256
ROPE_THETA = 10000.0

LANES = 128
SUBLANES = 8
VMEM_LIMIT = 48 * 1024 * 1024
FLT_MAX = float(np.finfo(np.float32).max)
FLT_TINY = float(np.finfo(np.float32).tiny)
SEARCH_PASS_CAP = 512
NEG_BIG = -1e30
LOG2_E = float(np.log2(np.e))
K_AUG = LANES
M_SLACK = 1.0 + 2.0 ** -5
CNT_ROWS = 4 * SUBLANES
L_FLOOR = 2.0 ** -80
BF16_SUBLANES = 16
PROJ_ROWS = 256
DSA_KEY_BLOCK = 512
V_ROWS = HEAD_DIM + BF16_SUBLANES


def _cparams(sem):
    return pltpu.CompilerParams(dimension_semantics=sem, vmem_limit_bytes=VMEM_LIMIT)


def _const_spec(shape):
    nd = len(shape)
    return pl.BlockSpec(shape, lambda *_: (0,) * nd)


def _swap_halves(z):
    lane = lax.broadcasted_iota(jnp.int32, z.shape, 1)
    lo_half = (lane % HEAD_DIM) < (HEAD_DIM // 2)
    return jnp.where(lo_half, pltpu.roll(z, LANES - HEAD_DIM // 2, 1), pltpu.roll(z, HEAD_DIM // 2, 1))


def _rope(z, cos, sin):
    outs = []
    for s in range(z.shape[1] // LANES):
        zs = z[:, s * LANES:(s + 1) * LANES]
        outs.append(zs * cos + _swap_halves(zs) * sin)
    return outs[0] if len(outs) == 1 else jnp.concatenate(outs, axis=1)


def _head_rmsnorm(z, gain, g):
    z2 = z * z
    hi = z2.astype(BF16)
    lo = (z2 - hi.astype(F32)).astype(BF16)
    ss = jnp.dot(hi, g, preferred_element_type=F32) + jnp.dot(lo, g, preferred_element_type=F32)
    return z * lax.rsqrt(ss * (1.0 / HEAD_DIM) + EPS) * gain


def _gelu_tanh(x):
    return 0.5 * x * (1.0 + jnp.tanh(np.sqrt(2.0 / np.pi).astype(np.float32) * (x + 0.044715 * (x * x * x))))


def _proj_kernel(x_ref, cos_ref, sin_ref, cprev_ref, hprev_ref,
                 nmix_ref, wa_ref, wq_ref, wkv_ref, wqi_ref, wkw_ref,
                 convw_ref, convb_ref, wrg_ref, wig_ref, brg_ref, big_ref, lam_ref,
                 qn_ref, kn_ref, g_ref,
                 ya_ref, k_ref, v_ref, ki_ref, tail_ref, hlast_ref, *rest, tb, pos0, wi_scale, dsa_layout):
    cbuf, hcar = rest[-2:]
    t = pl.program_id(1)
    w = cbuf.shape[1]

    @pl.when(t == 0)
    def _():
        cbuf[0:SUBLANES, :] = cprev_ref[0]
        hcar[...] = hprev_ref[0]

    x = x_ref[0]
    ms = jnp.mean(x * x, axis=-1, keepdims=True)
    h = (x * lax.rsqrt(ms + EPS) * nmix_ref[...]).astype(BF16)

    pa = jnp.dot(h, wa_ref[...], preferred_element_type=F32)
    xr = pa[:, :w]
    gate = pa[:, w:]
    cbuf[SUBLANES:SUBLANES + tb, :] = xr
    xc = convb_ref[...]
    for j in range(CONV_W):
        off = SUBLANES - (CONV_W - 1) + j
        xc = xc + cbuf[off:off + tb, :] * convw_ref[j:j + 1, :]
    tail = cbuf[tb:tb + SUBLANES, :]
    cbuf[0:SUBLANES, :] = tail
    tail_ref[0] = tail

    xcb = xc.astype(BF16)
    r = jax.nn.sigmoid(jnp.dot(xcb, wrg_ref[...], preferred_element_type=F32) + brg_ref[...])
    ig = jax.nn.sigmoid(jnp.dot(xcb, wig_ref[...], preferred_element_type=F32) + big_ref[...])
    nl = -lam_ref[...]
    softplus = jnp.maximum(nl, 0.0) + jnp.log(1.0 + jnp.exp(-jnp.abs(nl)))
    log_a = (-LRU_C) * r * softplus
    a = jnp.exp(log_a)
    mult = jnp.sqrt(1.0 - jnp.exp(2.0 * log_a))
    row = lax.broadcasted_iota(jnp.int32, (tb, 1), 0)
    mult = jnp.where(row + (pos0 + t * tb) == 0, 1.0, mult)
    b = mult * ig * xc
    s = 1
    while s < SUBLANES:
        keep = (row % SUBLANES) >= s
        a_s = jnp.where(keep, pltpu.roll(a, s, 0), 1.0)
        b_s = jnp.where(keep, pltpu.roll(b, s, 0), 0.0)
        b = a * b_s + b
        a = a * a_s
        s *= 2
    h_in = hcar[...]
    groups = []
    for gi in range(tb // SUBLANES):
        rows = slice(gi * SUBLANES, (gi + 1) * SUBLANES)
        hg = a[rows, :] * h_in + b[rows, :]
        groups.append(hg)
        h_in = hg[SUBLANES - 1:SUBLANES, :]
    hs = jnp.concatenate(groups, axis=0)
    hcar[...] = hs[tb - 1:tb, :]
    hlast_ref[0] = hs[tb - SUBLANES:tb, :]
    ya_ref[0] = (hs * _gelu_tanh(gate)).astype(ya_ref.dtype)

    cos = cos_ref[...]
    sin = sin_ref[...]
    g = g_ref[...]
    pq = jnp.dot(h, wq_ref[...], preferred_element_type=F32)
    q = _rope(_head_rmsnorm(pq, qn_ref[...], g), cos, sin) * (LOG2_E * HEAD_DIM ** -0.5)

    pkv = jnp.dot(h, wkv_ref[...], preferred_element_type=F32)
    kw_ = N_KV_HEADS * HEAD_DIM
    k = _rope(_head_rmsnorm(pkv[:, :kw_], kn_ref[...], g[:kw_, :kw_]), cos, sin)
    v = pkv[:, kw_:]
    k_ref[0] = k
    v_ref[0] = v

    pqi = jnp.dot(h, wqi_ref[...], preferred_element_type=F32)
    qi = _rope(pqi, cos, sin)

    pkw = jnp.dot(h, wkw_ref[...], preferred_element_type=F32)
    lane = lax.broadcasted_iota(jnp.int32, pkw.shape, 1)
    kwv = jnp.where(lane < IDX_DIM, _rope(pkw, cos, sin), pkw * wi_scale)
    ki_ref[0] = kwv[:, :IDX_DIM]

    if not dsa_layout:
        q_ref, qi_ref, kw_ref = rest[:3]
        q_ref[0] = q.astype(q_ref.dtype)
        qi_ref[0] = qi.astype(qi_ref.dtype)
        kw_ref[0] = kwv
        return

    qg_ref, qiT_ref, wiT_ref, kaug_ref, kib_ref, vT_ref = rest[:6]
    grp = N_HEADS // N_KV_HEADS
    qT = q.T.astype(qg_ref.dtype)
    qg_ref[0, 0] = jnp.concatenate(
        [jnp.concatenate([qT[(gk * grp + hq) * HEAD_DIM:(gk * grp + hq + 1) * HEAD_DIM, :]
                          for hq in range(grp)], axis=1) for gk in range(N_KV_HEADS)], axis=0)
    qiT_ref[0] = qi.T.astype(qiT_ref.dtype)
    wiT_ref[0] = kwv.T[IDX_DIM:IDX_DIM + N_IDX_HEADS, :]
    kib_ref[0] = kwv[:, :IDX_DIM].astype(kib_ref.dtype)
    one_lane = jnp.where(lax.broadcasted_iota(jnp.int32, (tb, K_AUG - HEAD_DIM), 1) == 0, 1.0, 0.0)
    kaug_ref[0] = jnp.concatenate(
        [piece for gk in range(N_KV_HEADS)
         for piece in (k[:, gk * HEAD_DIM:(gk + 1) * HEAD_DIM], one_lane)], axis=1).astype(kaug_ref.dtype)
    vT = v.T
    one_row = jnp.where(lax.broadcasted_iota(jnp.int32, (V_ROWS - HEAD_DIM, tb), 0) == 0, 1.0, 0.0)
    vT_ref[0, 0] = jnp.concatenate(
        [piece for gk in range(N_KV_HEADS)
         for piece in (vT[gk * HEAD_DIM:(gk + 1) * HEAD_DIM, :], one_row)], axis=0).astype(vT_ref.dtype)


def _proj_call(x, cos, sin, cprev, hprev, wts, *, tb, pos0, sk=None):
    bsz, t_len, d = x.shape
    w = wts["conv_b"].shape[1]
    aw = N_HEADS * HEAD_DIM
    kvw = 2 * N_KV_HEADS * HEAD_DIM
    nt = t_len // tb
    wi_scale = float(N_IDX_HEADS ** -0.5 * IDX_DIM ** -0.5)
    names = ("norm_mix", "w_a", "w_q", "w_kv", "w_qi", "w_kw", "conv_w", "conv_b",
             "w_rg", "w_ig", "b_rg", "b_ig", "lam", "q_norm", "k_norm", "g")
    warrs = [wts[n] for n in names]
    in_specs = [
        pl.BlockSpec((1, tb, d), lambda b, t: (b, t, 0)),
        pl.BlockSpec((tb, LANES), lambda b, t: (t, 0)),
        pl.BlockSpec((tb, LANES), lambda b, t: (t, 0)),
        pl.BlockSpec((1, SUBLANES, w), lambda b, t: (b, 0, 0)),
        pl.BlockSpec((1, 1, w), lambda b, t: (b, 0, 0)),
    ] + [_const_spec(a.shape) for a in warrs]
    out_shape = [
        jax.ShapeDtypeStruct((bsz, t_len, w), BF16),
        jax.ShapeDtypeStruct((bsz, t_len, kvw // 2), F32),
        jax.ShapeDtypeStruct((bsz, t_len, kvw // 2), F32),
        jax.ShapeDtypeStruct((bsz, t_len, IDX_DIM), F32),
        jax.ShapeDtypeStruct((bsz, SUBLANES, w), F32),
        jax.ShapeDtypeStruct((bsz, SUBLANES, w), F32),
    ]
    out_specs = [
        pl.BlockSpec((1, tb, w), lambda b, t: (b, t, 0)),
        pl.BlockSpec((1, tb, kvw // 2), lambda b, t: (b, t, 0)),
        pl.BlockSpec((1, tb, kvw // 2), lambda b, t: (b, t, 0)),
        pl.BlockSpec((1, tb, IDX_DIM), lambda b, t: (b, t, 0)),
        pl.BlockSpec((1, SUBLANES, w), lambda b, t: (b, 0, 0)),
        pl.BlockSpec((1, SUBLANES, w), lambda b, t: (b, 0, 0)),
    ]
    dsa_layout = sk is not None
    if dsa_layout:
        grp = N_HEADS // N_KV_HEADS
        per_sk = sk // tb
        out_shape += [
            jax.ShapeDtypeStruct((bsz, nt, kvw // 2, grp * tb), BF16),
            jax.ShapeDtypeStruct((bsz, aw, t_len), BF16),
            jax.ShapeDtypeStruct((bsz, N_IDX_HEADS, t_len), F32),
            jax.ShapeDtypeStruct((bsz, t_len, N_KV_HEADS * K_AUG), BF16),
            jax.ShapeDtypeStruct((bsz, t_len, IDX_DIM), BF16),
            jax.ShapeDtypeStruct((bsz, t_len // sk, N_KV_HEADS * V_ROWS, sk), BF16),
        ]
        out_specs += [
            pl.BlockSpec((1, 1, kvw // 2, grp * tb), lambda b, t: (b, t, 0, 0)),
            pl.BlockSpec((1, aw, tb), lambda b, t: (b, 0, t)),
            pl.BlockSpec((1, N_IDX_HEADS, tb), lambda b, t: (b, 0, t)),
            pl.BlockSpec((1, tb, N_KV_HEADS * K_AUG), lambda b, t: (b, t, 0)),
            pl.BlockSpec((1, tb, IDX_DIM), lambda b, t: (b, t, 0)),
            pl.BlockSpec((1, 1, N_KV_HEADS * V_ROWS, tb), lambda b, t: (b, t // per_sk, 0, t % per_sk)),
        ]
    else:
        out_shape += [jax.ShapeDtypeStruct((bsz, t_len, aw), BF16),
                      jax.ShapeDtypeStruct((bsz, t_len, aw), BF16),
                      jax.ShapeDtypeStruct((bsz, t_len, LANES), F32)]
        out_specs += [pl.BlockSpec((1, tb, aw), lambda b, t: (b, t, 0)),
                      pl.BlockSpec((1, tb, aw), lambda b, t: (b, t, 0)),
                      pl.BlockSpec((1, tb, LANES), lambda b, t: (b, t, 0))]
    return pl.pallas_call(
        functools.partial(_proj_kernel, tb=tb, pos0=pos0, wi_scale=wi_scale, dsa_layout=dsa_layout),
        grid=(bsz, nt),
        in_specs=in_specs,
        out_specs=out_specs,
        out_shape=out_shape,
        scratch_shapes=[pltpu.VMEM((SUBLANES + tb, w), F32), pltpu.VMEM((1, w), F32)],
        compiler_params=_cparams(("arbitrary", "arbitrary")),
        name="proj",
    )(x, cos, sin, cprev, hprev, *warrs)


def _for_blocks(n, body, init):
    def pair(jj, carry):
        return body(2 * jj + 1, body(2 * jj, carry))
    carry = lax.fori_loop(0, n // 2, pair, init)
    return lax.cond(n % 2 == 1, lambda c: body(n - 1, c), lambda c: c, carry)


def _dsa_kernel(qT_ref, qiT_ref, wiT_ref, k_ref, vT_ref, ki_ref, kmax_ref, o_ref,
                sc_scr, m_scr, acc_scr, qa_scr, *, tq, sk, s_real, n_kblocks, q_pos0, n_q_real, topk):
    i = pl.program_id(1)
    q_idx = i * tq + lax.broadcasted_iota(jnp.int32, (1, tq), 1)
    q_adm_end = ((q_pos0 + q_idx) // CHUNK + 1) * CHUNK
    q_adm_end = jnp.minimum(q_adm_end, s_real)
    last_q = jnp.minimum(i * tq + tq - 1, n_q_real - 1)
    blk_end = jnp.minimum(((q_pos0 + last_q) // CHUNK + 1) * CHUNK, s_real)
    nkb = jnp.minimum((blk_end + sk - 1) // sk, n_kblocks)

    def score_body(j, carry):
        smax, smin = carry
        off = pl.multiple_of(j * sk, sk)
        kib = ki_ref[0, pl.ds(off, sk), :]
        acc = jnp.zeros((sk, tq), F32)
        for hh in range(N_IDX_HEADS):
            sc = jnp.dot(kib, qiT_ref[0, hh * IDX_DIM:(hh + 1) * IDX_DIM, :], preferred_element_type=F32)
            acc = acc + jnp.maximum(sc, 0.0) * wiT_ref[0, hh:hh + 1, :]
        kidx = off + lax.broadcasted_iota(jnp.int32, (sk, 1), 0)
        adm = kidx < q_adm_end
        hi_part = jnp.where(adm, acc, -jnp.inf)
        lo_part = jnp.where(adm, acc, jnp.inf)
        sc_scr[pl.ds(off, sk), :] = hi_part
        return (jnp.maximum(smax, jnp.max(hi_part.reshape(sk // CNT_ROWS, CNT_ROWS, tq), axis=0)),
                jnp.minimum(smin, jnp.min(lo_part.reshape(sk // CNT_ROWS, CNT_ROWS, tq), axis=0)))

    smax, smin = _for_blocks(
        nkb, score_body,
        (jnp.full((CNT_ROWS, tq), -jnp.inf, F32), jnp.full((CNT_ROWS, tq), jnp.inf, F32)))
    smax = jnp.max(smax, axis=0, keepdims=True)
    smin = jnp.min(smin, axis=0, keepdims=True)

    def count(thr, strict=False):
        def body(j, cnt):
            off = pl.multiple_of(j * sk, sk)
            blk = sc_scr[pl.ds(off, sk), :]
            above = (blk > thr if strict else blk >= thr).astype(jnp.int32)
            return cnt + jnp.sum(above.reshape(sk // CNT_ROWS, CNT_ROWS, tq), axis=0)
        cnt = _for_blocks(nkb, body, jnp.zeros((CNT_ROWS, tq), jnp.int32))
        return jnp.sum(cnt, axis=0, keepdims=True)

    open0 = jnp.logical_and(q_adm_end > topk, q_idx < n_q_real).astype(jnp.int32)

    zero = jnp.zeros((1, tq), F32)
    c_nonneg = count(zero)
    c_pos = count(zero, strict=True)
    is_open0 = open0 > 0
    at_zero = jnp.logical_and(is_open0, jnp.logical_or(
        c_nonneg == topk, jnp.logical_and(c_pos < topk, c_nonneg > topk)))
    tie0 = jnp.logical_and(at_zero, c_nonneg > topk).astype(jnp.int32)
    thr0 = jnp.where(at_zero, 0.0, -FLT_MAX)
    open_a = jnp.where(at_zero, 0, open0)
    above_zero = c_pos >= topk
    lo0 = jnp.where(above_zero, 0.0, smin)
    hi0 = jnp.where(above_zero, smax * (1.0 + 2.0 ** -22) + FLT_TINY, 0.0)

    def search_cond(st):
        it, _, _, _, _, _, n_open = st
        return jnp.logical_and(it < SEARCH_PASS_CAP, n_open > 0)

    def search_body(st):
        it, lo, hi, thr, opn, tie, _ = st
        mid = 0.5 * lo + 0.5 * hi
        cnt = count(mid)
        is_open = opn > 0
        hit = jnp.logical_and(is_open, cnt == topk)
        stuck = jnp.logical_and(jnp.logical_and(is_open, cnt != topk),
                                jnp.logical_or(mid == lo, mid == hi))
        thr = jnp.where(hit, mid, jnp.where(stuck, lo, thr))
        tie = jnp.where(stuck, 1, tie)
        moving = jnp.logical_and(is_open, jnp.logical_not(jnp.logical_or(hit, stuck)))
        lo = jnp.where(jnp.logical_and(moving, cnt > topk), mid, lo)
        hi = jnp.where(jnp.logical_and(moving, cnt < topk), mid, hi)
        opn = jnp.where(jnp.logical_or(hit, stuck), 0, opn)
        return it + 1, lo, hi, thr, opn, tie, jnp.sum(opn)

    _, _, _, thr, _, tie, _ = lax.while_loop(
        search_cond, search_body, (jnp.int32(0), lo0, hi0, thr0, open_a, tie0, jnp.sum(open_a)))

    @pl.when(jnp.sum(tie) > 0)
    def _():
        cnt_gt = count(thr, strict=True)
        need = jnp.where(tie > 0, topk - cnt_gt, jnp.int32(2 ** 30)).astype(F32)
        r_i = lax.broadcasted_iota(jnp.int32, (sk, sk), 0)
        c_i = lax.broadcasted_iota(jnp.int32, (sk, sk), 1)
        tri = (c_i <= r_i).astype(BF16)

        def tie_body(j, seen):
            off = pl.multiple_of(j * sk, sk)
            blk = sc_scr[pl.ds(off, sk), :]
            eq = blk == thr
            rank = jnp.dot(tri, eq.astype(BF16), preferred_element_type=F32) + seen
            sc_scr[pl.ds(off, sk), :] = jnp.where(jnp.logical_and(eq, rank > need), -jnp.inf, blk)
            return rank[sk - 1:sk, :]

        lax.fori_loop(0, nkb, tie_body, jnp.zeros((1, tq), F32))

    grp = N_HEADS // N_KV_HEADS

    def finish():
        outs = []
        for gk in range(N_KV_HEADS):
            acc = acc_scr[gk]
            for hq in range(grp):
                sl = slice(hq * tq, (hq + 1) * tq)
                outs.append(acc[:HEAD_DIM, sl] / acc[HEAD_DIM:HEAD_DIM + 1, sl])
        o_ref[0] = jnp.concatenate(outs, axis=0).T.astype(o_ref.dtype)

    for gk in range(N_KV_HEADS):
        qb = qT_ref[0, 0, gk * HEAD_DIM:(gk + 1) * HEAD_DIM, :]
        qf = qb.astype(F32)
        qn2 = jnp.sum(qf * qf, axis=0, keepdims=True)
        m = jnp.sqrt(qn2 * kmax_ref[0, :, gk * K_AUG:gk * K_AUG + 1]) * M_SLACK
        row = lax.broadcasted_iota(jnp.int32, (BF16_SUBLANES, grp * tq), 0)
        qa_scr[gk, 0:HEAD_DIM, :] = qb
        qa_scr[gk, HEAD_DIM:HEAD_DIM + BF16_SUBLANES, :] = jnp.where(row == 0, -m, 0.0).astype(BF16)
        qa_scr[gk, HEAD_DIM + BF16_SUBLANES:, :] = jnp.zeros(
            (K_AUG - HEAD_DIM - BF16_SUBLANES, grp * tq), BF16)
    acc_scr[...] = jnp.zeros(acc_scr.shape, F32)

    def fast_body(j, carry):
        off = pl.multiple_of(j * sk, sk)
        sel = sc_scr[pl.ds(off, sk), :] >= thr
        logits = [jnp.dot(k_ref[0, pl.ds(off, sk), gk * K_AUG:(gk + 1) * K_AUG], qa_scr[gk],
                          preferred_element_type=F32) for gk in range(N_KV_HEADS)]
        for gk in range(N_KV_HEADS):
            s = logits[gk]
            p = jnp.concatenate(
                [jnp.exp2(jnp.where(sel, s[:, hq * tq:(hq + 1) * tq], NEG_BIG)) for hq in range(grp)],
                axis=1).astype(BF16)
            vtb = vT_ref[0, j, gk * V_ROWS:(gk + 1) * V_ROWS, :]
            acc_scr[gk] += jnp.dot(vtb, p, preferred_element_type=F32)
        return carry

    _for_blocks(nkb, fast_body, 0)
    l_min = jnp.min(acc_scr[:, HEAD_DIM:HEAD_DIM + 1, :])
    healthy = l_min > L_FLOOR

    @pl.when(healthy)
    def _():
        finish()

    @pl.when(jnp.logical_not(healthy))
    def _():
        m_scr[...] = jnp.full(m_scr.shape, NEG_BIG, F32)
        acc_scr[...] = jnp.zeros(acc_scr.shape, F32)

        def slow_body(j, carry):
            off = pl.multiple_of(j * sk, sk)
            sel = sc_scr[pl.ds(off, sk), :] >= thr
            for gk in range(N_KV_HEADS):
                kb = k_ref[0, pl.ds(off, sk), gk * K_AUG:gk * K_AUG + HEAD_DIM]
                vtb = vT_ref[0, j, gk * V_ROWS:(gk + 1) * V_ROWS, :]
                s = jnp.dot(kb, qT_ref[0, 0, gk * HEAD_DIM:(gk + 1) * HEAD_DIM, :],
                            preferred_element_type=F32)
                s = jnp.concatenate(
                    [jnp.where(sel, s[:, hq * tq:(hq + 1) * tq], NEG_BIG) for hq in range(grp)], axis=1)
                m_old = m_scr[gk]
                m_new = jnp.maximum(m_old, jnp.max(s, axis=0, keepdims=True))
                p = jnp.exp2(s - m_new).astype(BF16)
                alpha = jnp.exp2(m_old - m_new)
                acc_scr[gk] = alpha * acc_scr[gk] + jnp.dot(vtb, p, preferred_element_type=F32)
                m_scr[gk] = m_new
            return carry

        lax.fori_loop(0, nkb, slow_body, 0)
        finish()


def _kmax_kernel(k_ref, g_ref, o_ref):
    kf = k_ref[0].astype(F32)
    ss = jnp.dot((kf * kf).astype(BF16), g_ref[...], preferred_element_type=F32)
    o_ref[0] = jnp.max(ss, axis=0, keepdims=True)


def _kmax_call(k_b, g):
    bsz, s_pad, kw = k_b.shape
    return pl.pallas_call(
        _kmax_kernel,
        grid=(bsz,),
        in_specs=[pl.BlockSpec((1, s_pad, kw), lambda b: (b, 0, 0)), _const_spec(g.shape)],
        out_specs=pl.BlockSpec((1, 1, kw), lambda b: (b, 0, 0)),
        out_shape=jax.ShapeDtypeStruct((bsz, 1, kw), F32),
        compiler_params=_cparams(("arbitrary",)),
        name="kmax",
    )(k_b, g)


def _dsa_call(qg, qiT, wiT, k_aug, vT_blk, ki_b, kmax, *, tq, sk, s_real, q_pos0, n_q_real, topk):
    bsz, aw, tq_total = qiT.shape
    n_kblocks = vT_blk.shape[1]
    s_pad = k_aug.shape[1]
    nq = tq_total // tq
    grp = N_HEADS // N_KV_HEADS
    kern = functools.partial(_dsa_kernel, tq=tq, sk=sk, s_real=s_real, n_kblocks=n_kblocks,
                             q_pos0=q_pos0, n_q_real=n_q_real, topk=topk)
    return pl.pallas_call(
        kern,
        grid=(bsz, nq),
        in_specs=[
            pl.BlockSpec((1, 1, N_KV_HEADS * HEAD_DIM, grp * tq), lambda b, i: (b, i, 0, 0)),
            pl.BlockSpec((1, aw, tq), lambda b, i: (b, 0, i)),
            pl.BlockSpec((1, N_IDX_HEADS, tq), lambda b, i: (b, 0, i)),
            pl.BlockSpec((1, s_pad, N_KV_HEADS * K_AUG), lambda b, i: (b, 0, 0)),
            pl.BlockSpec((1, n_kblocks, N_KV_HEADS * V_ROWS, sk), lambda b, i: (b, 0, 0, 0)),
            pl.BlockSpec((1, s_pad, IDX_DIM), lambda b, i: (b, 0, 0)),
            pl.BlockSpec((1, 1, N_KV_HEADS * K_AUG), lambda b, i: (b, 0, 0)),
        ],
        out_specs=pl.BlockSpec((1, tq, aw), lambda b, i: (b, i, 0)),
        out_shape=jax.ShapeDtypeStruct((bsz, tq_total, aw), BF16),
        scratch_shapes=[
            pltpu.VMEM((s_pad, tq), F32),
            pltpu.VMEM((N_KV_HEADS, 1, grp * tq), F32),
            pltpu.VMEM((N_KV_HEADS, V_ROWS, grp * tq), F32),
            pltpu.VMEM((N_KV_HEADS, K_AUG, grp * tq), BF16),
        ],
        compiler_params=_cparams(("arbitrary", "arbitrary")),
        name="dsa",
    )(qg, qiT, wiT, k_aug, vT_blk, ki_b, kmax)


def _ffn_kernel(x_ref, ya_ref, yb_ref, woa_ref, wob_ref, nf_ref, wg_ref, wu_ref, wo_ref, o_ref, *, n_chunks):
    x1 = x_ref[...] + jnp.dot(ya_ref[...], woa_ref[...], preferred_element_type=F32)
    x1 = x1 + jnp.dot(yb_ref[...], wob_ref[...], preferred_element_type=F32)
    ms = jnp.mean(x1 * x1, axis=-1, keepdims=True)
    hf = (x1 * lax.rsqrt(ms + EPS) * nf_ref[...]).astype(BF16)

    def body(c, acc):
        gt = jnp.dot(hf, wg_ref[c], preferred_element_type=F32)
        up = jnp.dot(hf, wu_ref[c], preferred_element_type=F32)
        act = (gt * jax.nn.sigmoid(gt) * up).astype(BF16)
        return acc + jnp.dot(act, wo_ref[c], preferred_element_type=F32)

    o_ref[...] = lax.fori_loop(0, n_chunks, body, x1)


def _ffn_call(x2, ya2, yb2, wts, *, tm):
    n, d = x2.shape
    w = ya2.shape[1]
    aw = yb2.shape[1]
    n_chunks, _, fc = wts["w_g"].shape
    single = pl.Buffered(1)

    def wspec(shape):
        nd = len(shape)
        return pl.BlockSpec(shape, lambda *_: (0,) * nd, pipeline_mode=single)

    return pl.pallas_call(
        functools.partial(_ffn_kernel, n_chunks=n_chunks),
        grid=(n // tm,),
        in_specs=[
            pl.BlockSpec((tm, d), lambda r: (r, 0)),
            pl.BlockSpec((tm, w), lambda r: (r, 0)),
            pl.BlockSpec((tm, aw), lambda r: (r, 0)),
            wspec(wts["w_out_a"].shape), wspec(wts["w_out_b"].shape), wspec(wts["norm_ffn"].shape),
            wspec(wts["w_g"].shape), wspec(wts["w_u"].shape), wspec(wts["w_o"].shape),
        ],
        out_specs=pl.BlockSpec((tm, d), lambda r: (r, 0)),
        out_shape=jax.ShapeDtypeStruct((n, d), F32),
        compiler_params=_cparams(("arbitrary",)),
        name="ffn",
    )(x2, ya2, yb2, wts["w_out_a"], wts["w_out_b"], wts["norm_ffn"], wts["w_g"], wts["w_u"], wts["w_o"])


def _block_diag(wb):
    n, a, b = wb.shape
    eye = jnp.eye(n, dtype=wb.dtype)
    return (eye[:, None, :, None] * wb[:, :, None, :]).reshape(n * a, n * b)


def _ffn_chunk(d_ff):
    for c in (d_ff, 512, 256, 128):
        if d_ff % c == 0:
            return c
    return d_ff


def _prep_weights(norm_mix, w_in, conv_w, conv_b, w_rg, b_rg, w_ig, b_ig, lru_lambda,
                  q_norm, k_norm, w_out, norm_ffn, w_ffn_in, w_ffn_out):
    d = w_in.shape[0]
    w = conv_b.shape[0]
    aw = N_HEADS * HEAD_DIM
    kw = N_KV_HEADS * HEAD_DIM
    iw = N_IDX_HEADS * IDX_DIM
    o = np.cumsum([0, w, w, aw, kw, kw, iw, IDX_DIM, N_IDX_HEADS])
    wb = w_in.astype(BF16)
    w_kw = jnp.pad(wb[:, o[6]:o[8]], ((0, 0), (0, LANES - (IDX_DIM + N_IDX_HEADS))))
    d_ff = w_ffn_out.shape[0]
    fc = _ffn_chunk(d_ff)
    nck = d_ff // fc
    wfi = w_ffn_in.astype(BF16)
    hd_id = np.arange(aw) // HEAD_DIM
    aug_id = np.arange(N_KV_HEADS * K_AUG)
    return {
        "norm_mix": norm_mix.reshape(1, d),
        "w_a": wb[:, o[0]:o[2]],
        "w_q": wb[:, o[2]:o[3]],
        "w_kv": wb[:, o[3]:o[5]],
        "w_qi": wb[:, o[5]:o[6]],
        "w_kw": w_kw,
        "conv_w": jnp.pad(conv_w, ((0, SUBLANES - CONV_W), (0, 0))),
        "conv_b": conv_b.reshape(1, w),
        "w_rg": _block_diag(w_rg).astype(BF16),
        "w_ig": _block_diag(w_ig).astype(BF16),
        "b_rg": b_rg.reshape(1, w),
        "b_ig": b_ig.reshape(1, w),
        "lam": lru_lambda.reshape(1, w),
        "q_norm": jnp.tile(q_norm, N_HEADS).reshape(1, aw),
        "k_norm": jnp.tile(k_norm, N_KV_HEADS).reshape(1, kw),
        "g": jnp.asarray(hd_id[:, None] == hd_id[None, :], BF16),
        "g_aug": jnp.asarray((aug_id[:, None] // K_AUG == aug_id[None, :] // K_AUG)
                             & (aug_id[:, None] % K_AUG < HEAD_DIM), BF16),
        "w_out_a": w_out[:w].astype(BF16),
        "w_out_b": w_out[w:].astype(BF16),
        "norm_ffn": norm_ffn.reshape(1, d),
        "w_g": wfi[:, :d_ff].reshape(d, nck, fc).transpose(1, 0, 2),
        "w_u": wfi[:, d_ff:].reshape(d, nck, fc).transpose(1, 0, 2),
        "w_o": w_ffn_out.astype(BF16).reshape(nck, fc, d),
    }


def _rope_tables(pos):
    half = HEAD_DIM // 2
    inv = ROPE_THETA ** (-jnp.arange(half, dtype=F32) / half)
    ang = pos.astype(F32)[:, None] * inv[None, :]
    cos = jnp.cos(ang)
    sin = jnp.sin(ang)
    reps = LANES // HEAD_DIM
    return (jnp.tile(jnp.concatenate([cos, cos], axis=1), (1, reps)),
            jnp.tile(jnp.concatenate([-sin, sin], axis=1), (1, reps)))


def _round_up(n, m):
    return (n + m - 1) // m * m


def _dsa_operands(q, qi, wi, k_all, v_all, ki_all, *, tq, sk):
    bsz, t_len, _ = q.shape
    s_real = k_all.shape[1]
    tq_total = _round_up(t_len, tq)
    s_pad = _round_up(s_real, sk)
    nkb = s_pad // sk
    pad_q = ((0, 0), (0, tq_total - t_len), (0, 0))
    pad_s = ((0, 0), (0, s_pad - s_real), (0, 0))
    grp = N_HEADS // N_KV_HEADS
    qg = jnp.pad(q, pad_q).reshape(bsz, tq_total // tq, tq, N_KV_HEADS, grp, HEAD_DIM)
    qg = qg.transpose(0, 1, 3, 5, 4, 2).reshape(bsz, tq_total // tq, N_KV_HEADS * HEAD_DIM, grp * tq)
    qiT = jnp.pad(qi, pad_q).transpose(0, 2, 1)
    wiT = jnp.pad(wi, pad_q).transpose(0, 2, 1)
    ki_b = jnp.pad(ki_all.astype(BF16), pad_s)
    v_b = jnp.pad(v_all.astype(BF16), pad_s).reshape(bsz, nkb, sk, N_KV_HEADS, HEAD_DIM)
    vT = v_b.transpose(0, 1, 3, 4, 2)
    ones = jnp.ones((bsz, nkb, N_KV_HEADS, 1, sk), BF16)
    zeros = jnp.zeros((bsz, nkb, N_KV_HEADS, V_ROWS - HEAD_DIM - 1, sk), BF16)
    vT_blk = jnp.concatenate([vT, ones, zeros], axis=3).reshape(bsz, nkb, N_KV_HEADS * V_ROWS, sk)
    k4 = jnp.pad(k_all.astype(BF16), pad_s).reshape(bsz, s_pad, N_KV_HEADS, HEAD_DIM)
    k_aug = jnp.concatenate(
        [k4, jnp.ones((bsz, s_pad, N_KV_HEADS, 1), BF16),
         jnp.zeros((bsz, s_pad, N_KV_HEADS, K_AUG - HEAD_DIM - 1), BF16)], axis=3,
    ).reshape(bsz, s_pad, N_KV_HEADS * K_AUG)
    return qg, qiT, wiT, k_aug, vT_blk, ki_b


def _layer(x, pos0, conv_prev, h_prev, k_past, v_past, ki_past, wts):
    bsz, t_len, d = x.shape
    w = conv_prev.shape[2]
    kw = N_KV_HEADS * HEAD_DIM
    pos = pos0 + jnp.arange(t_len, dtype=jnp.int32)
    cos, sin = _rope_tables(pos)
    tb = PROJ_ROWS if t_len % PROJ_ROWS == 0 else t_len
    sk = DSA_KEY_BLOCK
    fused = k_past is None and t_len % sk == 0 and tb == PROJ_ROWS
    tq = tb if fused else LANES
    cprev = jnp.pad(conv_prev, ((0, 0), (SUBLANES - (CONV_W - 1), 0), (0, 0)))
    outs = _proj_call(x, cos, sin, cprev, h_prev.reshape(bsz, 1, w), wts, tb=tb, pos0=pos0,
                      sk=sk if fused else None)
    ya, k_new, v_new, ki_new, tail, hlast = outs[:6]
    if fused:
        s_real = t_len
        qg, qiT, wiT, k_aug, ki_b, vT_blk = outs[6:]
    else:
        q, qi, kwi = outs[6:]
        wi = kwi[..., IDX_DIM:IDX_DIM + N_IDX_HEADS]
        if k_past is None:
            k_all, v_all, ki_all = k_new, v_new, ki_new
        else:
            p_len = k_past.shape[1]
            k_all = jnp.concatenate([k_past.reshape(bsz, p_len, kw), k_new], axis=1)
            v_all = jnp.concatenate([v_past.reshape(bsz, p_len, kw), v_new], axis=1)
            ki_all = jnp.concatenate([ki_past, ki_new], axis=1)
        s_real = k_all.shape[1]
        qg, qiT, wiT, k_aug, vT_blk, ki_b = _dsa_operands(q, qi, wi, k_all, v_all, ki_all, tq=tq, sk=sk)
    topk = min(TOPK_MAX, s_real // 4)
    kmax = _kmax_call(k_aug, wts["g_aug"])
    yb = _dsa_call(qg, qiT, wiT, k_aug, vT_blk, ki_b, kmax, tq=tq, sk=sk, s_real=s_real,
                   q_pos0=pos0, n_q_real=t_len, topk=topk)[:, :t_len]

    n = bsz * t_len
    tm = 512 if n % 512 == 0 else n
    y = _ffn_call(x.reshape(n, d), ya.reshape(n, w), yb.reshape(n, N_HEADS * HEAD_DIM), wts, tm=tm)
    return (y.reshape(bsz, t_len, d),
            k_new.reshape(bsz, t_len, N_KV_HEADS, HEAD_DIM),
            v_new.reshape(bsz, t_len, N_KV_HEADS, HEAD_DIM),
            ki_new,
            hlast[:, SUBLANES - 1],
            tail[:, SUBLANES - (CONV_W - 1):])


def kernel(x_prompt, x_sample, cache_k, cache_v, cache_kidx, state_h, state_conv, norm_mix, w_in, conv_w, conv_b, w_rg, b_rg, w_ig, b_ig, lru_lambda, q_norm, k_norm, w_out, norm_ffn, w_ffn_in, w_ffn_out):
    wts = _prep_weights(norm_mix, w_in, conv_w, conv_b, w_rg, b_rg, w_ig, b_ig, lru_lambda,
                        q_norm, k_norm, w_out, norm_ffn, w_ffn_in, w_ffn_out)
    bp = x_prompt.shape[0]
    w = conv_b.shape[0]
    p_len = cache_k.shape[1]
    conv0 = jnp.zeros((bp, CONV_W - 1, w), x_prompt.dtype)
    h0 = jnp.zeros((bp, w), x_prompt.dtype)
    yp, k_p, v_p, ki_p, h_p, conv_p = _layer(x_prompt, 0, conv0, h0, None, None, None, wts)
    ys, k_s, v_s, ki_s, h_s, conv_s = _layer(x_sample, p_len, state_conv, state_h,
                                             cache_k, cache_v, cache_kidx, wts)
    return (yp, ys, k_p, v_p, ki_p, h_p, conv_p, k_s, v_s, ki_s, h_s, conv_s)
```

```python
import functools

import numpy as np
import jax
import jax.numpy as jnp
from jax import lax
from jax.experimental import pallas as pl
from jax.experimental.pallas import tpu as pltpu

F32 = jnp.float32
BF16 = jnp.bfloat16

CHUNK = 64
EPS = 1e-6
LRU_BLOCKS = 8
CONV_W = 4
LRU_C = 8.0
N_HEADS = 8
N_KV_HEADS = 2
HEAD_DIM = 64
N_IDX_HEADS = 8
IDX_DIM = 64
TOPK_MAX = 256
ROPE_THETA = 10000.0

LANES = 128
SUBLANES = 8
VMEM_LIMIT = 48 * 1024 * 1024
FLT_MAX = float(np.finfo(np.float32).max)
FLT_TINY = float(np.finfo(np.float32).tiny)
SEARCH_PASS_CAP = 512
NEG_BIG = -1e30
LOG2_E = float(np.log2(np.e))
K_AUG = LANES
M_SLACK = 1.0 + 2.0 ** -5
CNT_ROWS = 4 * SUBLANES
L_FLOOR = 2.0 ** -80
BF16_SUBLANES = 16
PROJ_ROWS = 256
DSA_KEY_BLOCK = 512
BLOCKS_PER_TRIP = 2
V_ROWS = HEAD_DIM + BF16_SUBLANES


def _cparams(sem):
    return pltpu.CompilerParams(dimension_semantics=sem, vmem_limit_bytes=VMEM_LIMIT)


def _const_spec(shape):
    nd = len(shape)
    return pl.BlockSpec(shape, lambda *_: (0,) * nd)


def _swap_halves(z):
    lane = lax.broadcasted_iota(jnp.int32, z.shape, 1)
    lo_half = (lane % HEAD_DIM) < (HEAD_DIM // 2)
    return jnp.where(lo_half, pltpu.roll(z, LANES - HEAD_DIM // 2, 1), pltpu.roll(z, HEAD_DIM // 2, 1))


def _rope(z, cos, sin):
    outs = []
    for s in range(z.shape[1] // LANES):
        zs = z[:, s * LANES:(s + 1) * LANES]
        outs.append(zs * cos + _swap_halves(zs) * sin)
    return outs[0] if len(outs) == 1 else jnp.concatenate(outs, axis=1)


def _head_rmsnorm(z, gain, g):
    z2 = z * z
    hi = z2.astype(BF16)
    lo = (z2 - hi.astype(F32)).astype(BF16)
    ss = jnp.dot(hi, g, preferred_element_type=F32) + jnp.dot(lo, g, preferred_element_type=F32)
    return z * lax.rsqrt(ss * (1.0 / HEAD_DIM) + EPS) * gain


def _gelu_tanh(x):
    return 0.5 * x * (1.0 + jnp.tanh(np.sqrt(2.0 / np.pi).astype(np.float32) * (x + 0.044715 * (x * x * x))))


def _proj_kernel(x_ref, cos_ref, sin_ref, cprev_ref, hprev_ref,
                 nmix_ref, wa_ref, wq_ref, wkv_ref, wqi_ref, wkw_ref,
                 convw_ref, convb_ref, wrg_ref, wig_ref, brg_ref, big_ref, lam_ref,
                 qn_ref, kn_ref, g_ref,
                 ya_ref, k_ref, v_ref, ki_ref, tail_ref, hlast_ref, *rest, tb, pos0, wi_scale, dsa_layout):
    cbuf, hcar = rest[-2:]
    t = pl.program_id(1)
    w = cbuf.shape[1]

    @pl.when(t == 0)
    def _():
        cbuf[0:SUBLANES, :] = cprev_ref[0]
        hcar[...] = hprev_ref[0]

    x = x_ref[0]
    ms = jnp.mean(x * x, axis=-1, keepdims=True)
    h = (x * lax.rsqrt(ms + EPS) * nmix_ref[...]).astype(BF16)

    pa = jnp.dot(h, wa_ref[...], preferred_element_type=F32)
    xr = pa[:, :w]
    gate = pa[:, w:]
    cbuf[SUBLANES:SUBLANES + tb, :] = xr
    xc = convb_ref[...]
    for j in range(CONV_W):
        off = SUBLANES - (CONV_W - 1) + j
        xc = xc + cbuf[off:off + tb, :] * convw_ref[j:j + 1, :]
    tail = cbuf[tb:tb + SUBLANES, :]
    cbuf[0:SUBLANES, :] = tail
    tail_ref[0] = tail

    xcb = xc.astype(BF16)
    r = jax.nn.sigmoid(jnp.dot(xcb, wrg_ref[...], preferred_element_type=F32) + brg_ref[...])
    ig = jax.nn.sigmoid(jnp.dot(xcb, wig_ref[...], preferred_element_type=F32) + big_ref[...])
    nl = -lam_ref[...]
    softplus = jnp.maximum(nl, 0.0) + jnp.log(1.0 + jnp.exp(-jnp.abs(nl)))
    log_a = (-LRU_C) * r * softplus
    a = jnp.exp(log_a)
    mult = jnp.sqrt(1.0 - jnp.exp(2.0 * log_a))
    row = lax.broadcasted_iota(jnp.int32, (tb, 1), 0)
    mult = jnp.where(row + (pos0 + t * tb) == 0, 1.0, mult)
    b = mult * ig * xc
    s = 1
    while s < SUBLANES:
        keep = (row % SUBLANES) >= s
        a_s = jnp.where(keep, pltpu.roll(a, s, 0), 1.0)
        b_s = jnp.where(keep, pltpu.roll(b, s, 0), 0.0)
        b = a * b_s + b
        a = a * a_s
        s *= 2
    h_in = hcar[...]
    groups = []
    for gi in range(tb // SUBLANES):
        rows = slice(gi * SUBLANES, (gi + 1) * SUBLANES)
        hg = a[rows, :] * h_in + b[rows, :]
        groups.append(hg)
        h_in = hg[SUBLANES - 1:SUBLANES, :]
    hs = jnp.concatenate(groups, axis=0)
    hcar[...] = hs[tb - 1:tb, :]
    hlast_ref[0] = hs[tb - SUBLANES:tb, :]
    ya_ref[0] = (hs * _gelu_tanh(gate)).astype(ya_ref.dtype)

    cos = cos_ref[...]
    sin = sin_ref[...]
    g = g_ref[...]
    pq = jnp.dot(h, wq_ref[...], preferred_element_type=F32)
    q = _rope(_head_rmsnorm(pq, qn_ref[...], g), cos, sin) * (LOG2_E * HEAD_DIM ** -0.5)

    pkv = jnp.dot(h, wkv_ref[...], preferred_element_type=F32)
    kw_ = N_KV_HEADS * HEAD_DIM
    k = _rope(_head_rmsnorm(pkv[:, :kw_], kn_ref[...], g[:kw_, :kw_]), cos, sin)
    v = pkv[:, kw_:]
    k_ref[0] = k
    v_ref[0] = v

    pqi = jnp.dot(h, wqi_ref[...], preferred_element_type=F32)
    qi = _rope(pqi, cos, sin)

    pkw = jnp.dot(h, wkw_ref[...], preferred_element_type=F32)
    lane = lax.broadcasted_iota(jnp.int32, pkw.shape, 1)
    kwv = jnp.where(lane < IDX_DIM, _rope(pkw, cos, sin), pkw * wi_scale)
    ki_ref[0] = kwv[:, :IDX_DIM]

    if not dsa_layout:
        q_ref, qi_ref, kw_ref = rest[:3]
        q_ref[0] = q.astype(q_ref.dtype)
        qi_ref[0] = qi.astype(qi_ref.dtype)
        kw_ref[0] = kwv
        return

    qg_ref, qiT_ref, wiT_ref, kaug_ref, kib_ref, vT_ref = rest[:6]
    grp = N_HEADS // N_KV_HEADS
    qT = q.T.astype(qg_ref.dtype)
    qg_ref[0, 0] = jnp.concatenate(
        [jnp.concatenate([qT[(gk * grp + hq) * HEAD_DIM:(gk * grp + hq + 1) * HEAD_DIM, :]
                          for hq in range(grp)], axis=1) for gk in range(N_KV_HEADS)], axis=0)
    qiT_ref[0] = qi.T.astype(qiT_ref.dtype)
    wiT_ref[0] = kwv.T[IDX_DIM:IDX_DIM + N_IDX_HEADS, :]
    kib_ref[0] = kwv[:, :IDX_DIM].astype(kib_ref.dtype)
    one_lane = jnp.where(lax.broadcasted_iota(jnp.int32, (tb, K_AUG - HEAD_DIM), 1) == 0, 1.0, 0.0)
    kaug_ref[0] = jnp.concatenate(
        [piece for gk in range(N_KV_HEADS)
         for piece in (k[:, gk * HEAD_DIM:(gk + 1) * HEAD_DIM], one_lane)], axis=1).astype(kaug_ref.dtype)
    vT = v.T
    one_row = jnp.where(lax.broadcasted_iota(jnp.int32, (V_ROWS - HEAD_DIM, tb), 0) == 0, 1.0, 0.0)
    vT_ref[0, 0] = jnp.concatenate(
        [piece for gk in range(N_KV_HEADS)
         for piece in (vT[gk * HEAD_DIM:(gk + 1) * HEAD_DIM, :], one_row)], axis=0).astype(vT_ref.dtype)


def _proj_call(x, cos, sin, cprev, hprev, wts, *, tb, pos0, sk=None):
    bsz, t_len, d = x.shape
    w = wts["conv_b"].shape[1]
    aw = N_HEADS * HEAD_DIM
    kvw = 2 * N_KV_HEADS * HEAD_DIM
    nt = t_len // tb
    wi_scale = float(N_IDX_HEADS ** -0.5 * IDX_DIM ** -0.5)
    names = ("norm_mix", "w_a", "w_q", "w_kv", "w_qi", "w_kw", "conv_w", "conv_b",
             "w_rg", "w_ig", "b_rg", "b_ig", "lam", "q_norm", "k_norm", "g")
    warrs = [wts[n] for n in names]
    in_specs = [
        pl.BlockSpec((1, tb, d), lambda b, t: (b, t, 0)),
        pl.BlockSpec((tb, LANES), lambda b, t: (t, 0)),
        pl.BlockSpec((tb, LANES), lambda b, t: (t, 0)),
        pl.BlockSpec((1, SUBLANES, w), lambda b, t: (b, 0, 0)),
        pl.BlockSpec((1, 1, w), lambda b, t: (b, 0, 0)),
    ] + [_const_spec(a.shape) for a in warrs]
    out_shape = [
        jax.ShapeDtypeStruct((bsz, t_len, w), BF16),
        jax.ShapeDtypeStruct((bsz, t_len, kvw // 2), F32),
        jax.ShapeDtypeStruct((bsz, t_len, kvw // 2), F32),
        jax.ShapeDtypeStruct((bsz, t_len, IDX_DIM), F32),
        jax.ShapeDtypeStruct((bsz, SUBLANES, w), F32),
        jax.ShapeDtypeStruct((bsz, SUBLANES, w), F32),
    ]
    out_specs = [
        pl.BlockSpec((1, tb, w), lambda b, t: (b, t, 0)),
        pl.BlockSpec((1, tb, kvw // 2), lambda b, t: (b, t, 0)),
        pl.BlockSpec((1, tb, kvw // 2), lambda b, t: (b, t, 0)),
        pl.BlockSpec((1, tb, IDX_DIM), lambda b, t: (b, t, 0)),
        pl.BlockSpec((1, SUBLANES, w), lambda b, t: (b, 0, 0)),
        pl.BlockSpec((1, SUBLANES, w), lambda b, t: (b, 0, 0)),
    ]
    dsa_layout = sk is not None
    if dsa_layout:
        grp = N_HEADS // N_KV_HEADS
        per_sk = sk // tb
        out_shape += [
            jax.ShapeDtypeStruct((bsz, nt, kvw // 2, grp * tb), BF16),
            jax.ShapeDtypeStruct((bsz, aw, t_len), BF16),
            jax.ShapeDtypeStruct((bsz, N_IDX_HEADS, t_len), F32),
            jax.ShapeDtypeStruct((bsz, t_len, N_KV_HEADS * K_AUG), BF16),
            jax.ShapeDtypeStruct((bsz, t_len, IDX_DIM), BF16),
            jax.ShapeDtypeStruct((bsz, t_len // sk, N_KV_HEADS * V_ROWS, sk), BF16),
        ]
        out_specs += [
            pl.BlockSpec((1, 1, kvw // 2, grp * tb), lambda b, t: (b, t, 0, 0)),
            pl.BlockSpec((1, aw, tb), lambda b, t: (b, 0, t)),
            pl.BlockSpec((1, N_IDX_HEADS, tb), lambda b, t: (b, 0, t)),
            pl.BlockSpec((1, tb, N_KV_HEADS * K_AUG), lambda b, t: (b, t, 0)),
            pl.BlockSpec((1, tb, IDX_DIM), lambda b, t: (b, t, 0)),
            pl.BlockSpec((1, 1, N_KV_HEADS * V_ROWS, tb), lambda b, t: (b, t // per_sk, 0, t % per_sk)),
        ]
    else:
        out_shape += [jax.ShapeDtypeStruct((bsz, t_len, aw), BF16),
                      jax.ShapeDtypeStruct((bsz, t_len, aw), BF16),
                      jax.ShapeDtypeStruct((bsz, t_len, LANES), F32)]
        out_specs += [pl.BlockSpec((1, tb, aw), lambda b, t: (b, t, 0)),
                      pl.BlockSpec((1, tb, aw), lambda b, t: (b, t, 0)),
                      pl.BlockSpec((1, tb, LANES), lambda b, t: (b, t, 0))]
    return pl.pallas_call(
        functools.partial(_proj_kernel, tb=tb, pos0=pos0, wi_scale=wi_scale, dsa_layout=dsa_layout),
        grid=(bsz, nt),
        in_specs=in_specs,
        out_specs=out_specs,
        out_shape=out_shape,
        scratch_shapes=[pltpu.VMEM((SUBLANES + tb, w), F32), pltpu.VMEM((1, w), F32)],
        compiler_params=_cparams(("arbitrary", "arbitrary")),
        name="proj",
    )(x, cos, sin, cprev, hprev, *warrs)


def _for_blocks(n, body, init):
    def trip(jj, carry):
        for u in range(BLOCKS_PER_TRIP):
            carry = body(BLOCKS_PER_TRIP * jj + u, carry)
        return carry
    whole = n // BLOCKS_PER_TRIP
    carry = lax.fori_loop(0, whole, trip, init)
    return lax.fori_loop(whole * BLOCKS_PER_TRIP, n, body, carry)


def _dsa_kernel(qT_ref, qiT_ref, wiT_ref, k_ref, vT_ref, ki_ref, kmax_ref, o_ref,
                sc_scr, m_scr, acc_scr, qa_scr, *, tq, sk, s_real, n_kblocks, q_pos0, n_q_real, topk):
    i = pl.program_id(1)
    q_idx = i * tq + lax.broadcasted_iota(jnp.int32, (1, tq), 1)
    q_adm_end = ((q_pos0 + q_idx) // CHUNK + 1) * CHUNK
    q_adm_end = jnp.minimum(q_adm_end, s_real)
    last_q = jnp.minimum(i * tq + tq - 1, n_q_real - 1)
    blk_end = jnp.minimum(((q_pos0 + last_q) // CHUNK + 1) * CHUNK, s_real)
    nkb = jnp.minimum((blk_end + sk - 1) // sk, n_kblocks)

    def score_body(j, carry):
        smax, smin = carry
        off = pl.multiple_of(j * sk, sk)
        kib = ki_ref[0, pl.ds(off, sk), :]
        acc = jnp.zeros((sk, tq), F32)
        for hh in range(N_IDX_HEADS):
            sc = jnp.dot(kib, qiT_ref[0, hh * IDX_DIM:(hh + 1) * IDX_DIM, :], preferred_element_type=F32)
            acc = acc + jnp.maximum(sc, 0.0) * wiT_ref[0, hh:hh + 1, :]
        kidx = off + lax.broadcasted_iota(jnp.int32, (sk, 1), 0)
        adm = kidx < q_adm_end
        hi_part = jnp.where(adm, acc, -jnp.inf)
        lo_part = jnp.where(adm, acc, jnp.inf)
        sc_scr[pl.ds(off, sk), :] = hi_part
        return (jnp.maximum(smax, jnp.max(hi_part.reshape(sk // CNT_ROWS, CNT_ROWS, tq), axis=0)),
                jnp.minimum(smin, jnp.min(lo_part.reshape(sk // CNT_ROWS, CNT_ROWS, tq), axis=0)))

    smax, smin = _for_blocks(
        nkb, score_body,
        (jnp.full((CNT_ROWS, tq), -jnp.inf, F32), jnp.full((CNT_ROWS, tq), jnp.inf, F32)))
    smax = jnp.max(smax, axis=0, keepdims=True)
    smin = jnp.min(smin, axis=0, keepdims=True)

    def count(thr, strict=False):
        def body(j, cnt):
            off = pl.multiple_of(j * sk, sk)
            blk = sc_scr[pl.ds(off, sk), :]
            above = (blk > thr if strict else blk >= thr).astype(jnp.int32)
            return cnt + jnp.sum(above.reshape(sk // CNT_ROWS, CNT_ROWS, tq), axis=0)
        cnt = _for_blocks(nkb, body, jnp.zeros((CNT_ROWS, tq), jnp.int32))
        return jnp.sum(cnt, axis=0, keepdims=True)

    open0 = jnp.logical_and(q_adm_end > topk, q_idx < n_q_real).astype(jnp.int32)

    zero = jnp.zeros((1, tq), F32)
    c_nonneg = count(zero)
    c_pos = count(zero, strict=True)
    is_open0 = open0 > 0
    at_zero = jnp.logical_and(is_open0, jnp.logical_or(
        c_nonneg == topk, jnp.logical_and(c_pos < topk, c_nonneg > topk)))
    tie0 = jnp.logical_and(at_zero, c_nonneg > topk).astype(jnp.int32)
    thr0 = jnp.where(at_zero, 0.0, -FLT_MAX)
    open_a = jnp.where(at_zero, 0, open0)
    above_zero = c_pos >= topk
    lo0 = jnp.where(above_zero, 0.0, smin)
    hi0 = jnp.where(above_zero, smax * (1.0 + 2.0 ** -22) + FLT_TINY, 0.0)

    def search_cond(st):
        it, _, _, _, _, _, n_open = st
        return jnp.logical_and(it < SEARCH_PASS_CAP, n_open > 0)

    def search_body(st):
        it, lo, hi, thr, opn, tie, _ = st
        mid = 0.5 * lo + 0.5 * hi
        cnt = count(mid)
        is_open = opn > 0
        hit = jnp.logical_and(is_open, cnt == topk)
        stuck = jnp.logical_and(jnp.logical_and(is_open, cnt != topk),
                                jnp.logical_or(mid == lo, mid == hi))
        thr = jnp.where(hit, mid, jnp.where(stuck, lo, thr))
        tie = jnp.where(stuck, 1, tie)
        moving = jnp.logical_and(is_open, jnp.logical_not(jnp.logical_or(hit, stuck)))
        lo = jnp.where(jnp.logical_and(moving, cnt > topk), mid, lo)
        hi = jnp.where(jnp.logical_and(moving, cnt < topk), mid, hi)
        opn = jnp.where(jnp.logical_or(hit, stuck), 0, opn)
        return it + 1, lo, hi, thr, opn, tie, jnp.sum(opn)

    _, _, _, thr, _, tie, _ = lax.while_loop(
        search_cond, search_body, (jnp.int32(0), lo0, hi0, thr0, open_a, tie0, jnp.sum(open_a)))

    @pl.when(jnp.sum(tie) > 0)
    def _():
        cnt_gt = count(thr, strict=True)
        need = jnp.where(tie > 0, topk - cnt_gt, jnp.int32(2 ** 30)).astype(F32)
        half = sk // 2
        r_i = lax.broadcasted_iota(jnp.int32, (half, half), 0)
        c_i = lax.broadcasted_iota(jnp.int32, (half, half), 1)
        tri = (c_i <= r_i).astype(BF16)

        def tie_body(j, seen):
            off = pl.multiple_of(j * sk, sk)
            blk = sc_scr[pl.ds(off, sk), :]
            eq = blk == thr
            eqb = eq.astype(BF16)
            r_top = jnp.dot(tri, eqb[:half], preferred_element_type=F32) + seen
            r_bot = jnp.dot(tri, eqb[half:], preferred_element_type=F32) + r_top[half - 1:half, :]
            rank = jnp.concatenate([r_top, r_bot], axis=0)
            sc_scr[pl.ds(off, sk), :] = jnp.where(jnp.logical_and(eq, rank > need), -jnp.inf, blk)
            return r_bot[half - 1:half, :]

        _for_blocks(nkb, tie_body, jnp.zeros((1, tq), F32))

    grp = N_HEADS // N_KV_HEADS

    def finish():
        outs = []
        for gk in range(N_KV_HEADS):
            acc = acc_scr[gk]
            for hq in range(grp):
                sl = slice(hq * tq, (hq + 1) * tq)
                outs.append(acc[:HEAD_DIM, sl] / acc[HEAD_DIM:HEAD_DIM + 1, sl])
        o_ref[0] = jnp.concatenate(outs, axis=0).T.astype(o_ref.dtype)

    for gk in range(N_KV_HEADS):
        qb = qT_ref[0, 0, gk * HEAD_DIM:(gk + 1) * HEAD_DIM, :]
        qf = qb.astype(F32)
        qn2 = jnp.sum(qf * qf, axis=0, keepdims=True)
        m = jnp.sqrt(qn2 * kmax_ref[0, :, gk * K_AUG:gk * K_AUG + 1]) * M_SLACK
        row = lax.broadcasted_iota(jnp.int32, (BF16_SUBLANES, grp * tq), 0)
        qa_scr[gk, 0:HEAD_DIM, :] = qb
        qa_scr[gk, HEAD_DIM:HEAD_DIM + BF16_SUBLANES, :] = jnp.where(row == 0, -m, 0.0).astype(BF16)
        qa_scr[gk, HEAD_DIM + BF16_SUBLANES:, :] = jnp.zeros(
            (K_AUG - HEAD_DIM - BF16_SUBLANES, grp * tq), BF16)
    acc_scr[...] = jnp.zeros(acc_scr.shape, F32)

    def fast_body(j, carry):
        off = pl.multiple_of(j * sk, sk)
        sel = sc_scr[pl.ds(off, sk), :] >= thr
        logits = [jnp.dot(k_ref[0, pl.ds(off, sk), gk * K_AUG:(gk + 1) * K_AUG], qa_scr[gk],
                          preferred_element_type=F32) for gk in range(N_KV_HEADS)]
        for gk in range(N_KV_HEADS):
            s = logits[gk]
            p = jnp.concatenate(
                [jnp.exp2(jnp.where(sel, s[:, hq * tq:(hq + 1) * tq], NEG_BIG)) for hq in range(grp)],
                axis=1).astype(BF16)
            vtb = vT_ref[0, j, gk * V_ROWS:(gk + 1) * V_ROWS, :]
            acc_scr[gk] += jnp.dot(vtb, p, preferred_element_type=F32)
        return carry

    _for_blocks(nkb, fast_body, 0)
    l_min = jnp.min(acc_scr[:, HEAD_DIM:HEAD_DIM + 1, :])
    healthy = l_min > L_FLOOR

    @pl.when(healthy)
    def _():
        finish()

    @pl.when(jnp.logical_not(healthy))
    def _():
        m_scr[...] = jnp.full(m_scr.shape, NEG_BIG, F32)
        acc_scr[...] = jnp.zeros(acc_scr.shape, F32)

        def slow_body(j, carry):
            off = pl.multiple_of(j * sk, sk)
            sel = sc_scr[pl.ds(off, sk), :] >= thr
            for gk in range(N_KV_HEADS):
                kb = k_ref[0, pl.ds(off, sk), gk * K_AUG:gk * K_AUG + HEAD_DIM]
                vtb = vT_ref[0, j, gk * V_ROWS:(gk + 1) * V_ROWS, :]
                s = jnp.dot(kb, qT_ref[0, 0, gk * HEAD_DIM:(gk + 1) * HEAD_DIM, :],
                            preferred_element_type=F32)
                s = jnp.concatenate(
                    [jnp.where(sel, s[:, hq * tq:(hq + 1) * tq], NEG_BIG) for hq in range(grp)], axis=1)
                m_old = m_scr[gk]
                m_new = jnp.maximum(m_old, jnp.max(s, axis=0, keepdims=True))
                p = jnp.exp2(s - m_new).astype(BF16)
                alpha = jnp.exp2(m_old - m_new)
                acc_scr[gk] = alpha * acc_scr[gk] + jnp.dot(vtb, p, preferred_element_type=F32)
                m_scr[gk] = m_new
            return carry

        lax.fori_loop(0, nkb, slow_body, 0)
        finish()


def _kmax_kernel(k_ref, g_ref, o_ref):
    kf = k_ref[0].astype(F32)
    ss = jnp.dot((kf * kf).astype(BF16), g_ref[...], preferred_element_type=F32)
    o_ref[0] = jnp.max(ss, axis=0, keepdims=True)


def _kmax_call(k_b, g):
    bsz, s_pad, kw = k_b.shape
    return pl.pallas_call(
        _kmax_kernel,
        grid=(bsz,),
        in_specs=[pl.BlockSpec((1, s_pad, kw), lambda b: (b, 0, 0)), _const_spec(g.shape)],
        out_specs=pl.BlockSpec((1, 1, kw), lambda b: (b, 0, 0)),
        out_shape=jax.ShapeDtypeStruct((bsz, 1, kw), F32),
        compiler_params=_cparams(("arbitrary",)),
        name="kmax",
    )(k_b, g)


def _keys_kernel(ck_ref, cv_ref, cki_ref, nk_ref, nv_ref, nki_ref, kaug_ref, kib_ref, vT_ref, *, n_cache_blocks):
    j = pl.program_id(1)
    sk = ck_ref.shape[1]
    t_new = nk_ref.shape[1]

    def cached_or_new(c_ref, n_ref):
        new = jnp.concatenate([n_ref[0], jnp.zeros((sk - t_new, n_ref.shape[2]), F32)], axis=0)
        return jnp.where(j < n_cache_blocks, c_ref[0], new)

    k = cached_or_new(ck_ref, nk_ref)
    v = cached_or_new(cv_ref, nv_ref)
    kib_ref[0] = cached_or_new(cki_ref, nki_ref).astype(kib_ref.dtype)
    one_lane = jnp.where(lax.broadcasted_iota(jnp.int32, (sk, K_AUG - HEAD_DIM), 1) == 0, 1.0, 0.0)
    kaug_ref[0] = jnp.concatenate(
        [piece for gk in range(N_KV_HEADS)
         for piece in (k[:, gk * HEAD_DIM:(gk + 1) * HEAD_DIM], one_lane)], axis=1).astype(kaug_ref.dtype)
    vT = v.T
    one_row = jnp.where(lax.broadcasted_iota(jnp.int32, (V_ROWS - HEAD_DIM, sk), 0) == 0, 1.0, 0.0)
    vT_ref[0, 0] = jnp.concatenate(
        [piece for gk in range(N_KV_HEADS)
         for piece in (vT[gk * HEAD_DIM:(gk + 1) * HEAD_DIM, :], one_row)], axis=0).astype(vT_ref.dtype)


def _keys_call(cache_k, cache_v, cache_ki, k_new, v_new, ki_new, *, sk):
    bsz, p_len, kw = cache_k.shape
    t_new = k_new.shape[1]
    ncb = p_len // sk
    nkb = ncb + 1

    def cache_spec(width):
        return pl.BlockSpec((1, sk, width), lambda b, j: (b, jnp.minimum(j, ncb - 1), 0))

    def new_spec(width):
        return pl.BlockSpec((1, t_new, width), lambda b, j: (b, 0, 0))

    return pl.pallas_call(
        functools.partial(_keys_kernel, n_cache_blocks=ncb),
        grid=(bsz, nkb),
        in_specs=[cache_spec(kw), cache_spec(kw), cache_spec(IDX_DIM),
                  new_spec(kw), new_spec(kw), new_spec(IDX_DIM)],
        out_specs=(
            pl.BlockSpec((1, sk, N_KV_HEADS * K_AUG), lambda b, j: (b, j, 0)),
            pl.BlockSpec((1, sk, IDX_DIM), lambda b, j: (b, j, 0)),
            pl.BlockSpec((1, 1, N_KV_HEADS * V_ROWS, sk), lambda b, j: (b, j, 0, 0)),
        ),
        out_shape=(
            jax.ShapeDtypeStruct((bsz, nkb * sk, N_KV_HEADS * K_AUG), BF16),
            jax.ShapeDtypeStruct((bsz, nkb * sk, IDX_DIM), BF16),
            jax.ShapeDtypeStruct((bsz, nkb, N_KV_HEADS * V_ROWS, sk), BF16),
        ),
        compiler_params=_cparams(("arbitrary", "arbitrary")),
        name="keys",
    )(cache_k, cache_v, cache_ki, k_new, v_new, ki_new)


def _dsa_call(qg, qiT, wiT, k_aug, vT_blk, ki_b, kmax, *, tq, sk, s_real, q_pos0, n_q_real, topk):
    bsz, aw, tq_total = qiT.shape
    n_kblocks = vT_blk.shape[1]
    s_pad = k_aug.shape[1]
    nq = tq_total // tq
    grp = N_HEADS // N_KV_HEADS
    kern = functools.partial(_dsa_kernel, tq=tq, sk=sk, s_real=s_real, n_kblocks=n_kblocks,
                             q_pos0=q_pos0, n_q_real=n_q_real, topk=topk)
    return pl.pallas_call(
        kern,
        grid=(bsz, nq),
        in_specs=[
            pl.BlockSpec((1, 1, N_KV_HEADS * HEAD_DIM, grp * tq), lambda b, i: (b, i, 0, 0)),
            pl.BlockSpec((1, aw, tq), lambda b, i: (b, 0, i)),
            pl.BlockSpec((1, N_IDX_HEADS, tq), lambda b, i: (b, 0, i)),
            pl.BlockSpec((1, s_pad, N_KV_HEADS * K_AUG), lambda b, i: (b, 0, 0)),
            pl.BlockSpec((1, n_kblocks, N_KV_HEADS * V_ROWS, sk), lambda b, i: (b, 0, 0, 0)),
            pl.BlockSpec((1, s_pad, IDX_DIM), lambda b, i: (b, 0, 0)),
            pl.BlockSpec((1, 1, N_KV_HEADS * K_AUG), lambda b, i: (b, 0, 0)),
        ],
        out_specs=pl.BlockSpec((1, tq, aw), lambda b, i: (b, i, 0)),
        out_shape=jax.ShapeDtypeStruct((bsz, tq_total, aw), BF16),
        scratch_shapes=[
            pltpu.VMEM((s_pad, tq), F32),
            pltpu.VMEM((N_KV_HEADS, 1, grp * tq), F32),
            pltpu.VMEM((N_KV_HEADS, V_ROWS, grp * tq), F32),
            pltpu.VMEM((N_KV_HEADS, K_AUG, grp * tq), BF16),
        ],
        compiler_params=_cparams(("arbitrary", "arbitrary")),
        name="dsa",
    )(qg, qiT, wiT, k_aug, vT_blk, ki_b, kmax)


def _ffn_kernel(x_ref, ya_ref, yb_ref, woa_ref, wob_ref, nf_ref, wg_ref, wu_ref, wo_ref, o_ref, *, n_chunks):
    x1 = x_ref[...] + jnp.dot(ya_ref[...], woa_ref[...], preferred_element_type=F32)
    x1 = x1 + jnp.dot(yb_ref[...], wob_ref[...], preferred_element_type=F32)
    ms = jnp.mean(x1 * x1, axis=-1, keepdims=True)
    hf = (x1 * lax.rsqrt(ms + EPS) * nf_ref[...]).astype(BF16)

    def body(c, acc):
        gt = jnp.dot(hf, wg_ref[c], preferred_element_type=F32)
        up = jnp.dot(hf, wu_ref[c], preferred_element_type=F32)
        act = (gt * jax.nn.sigmoid(gt) * up).astype(BF16)
        return acc + jnp.dot(act, wo_ref[c], preferred_element_type=F32)

    o_ref[...] = lax.fori_loop(0, n_chunks, body, x1)


def _ffn_call(x2, ya2, yb2, wts, *, tm):
    n, d = x2.shape
    w = ya2.shape[1]
    aw = yb2.shape[1]
    n_chunks, _, fc = wts["w_g"].shape
    single = pl.Buffered(1)

    def wspec(shape):
        nd = len(shape)
        return pl.BlockSpec(shape, lambda *_: (0,) * nd, pipeline_mode=single)

    return pl.pallas_call(
        functools.partial(_ffn_kernel, n_chunks=n_chunks),
        grid=(n // tm,),
        in_specs=[
            pl.BlockSpec((tm, d), lambda r: (r, 0)),
            pl.BlockSpec((tm, w), lambda r: (r, 0)),
            pl.BlockSpec((tm, aw), lambda r: (r, 0)),
            wspec(wts["w_out_a"].shape), wspec(wts["w_out_b"].shape), wspec(wts["norm_ffn"].shape),
            wspec(wts["w_g"].shape), wspec(wts["w_u"].shape), wspec(wts["w_o"].shape),
        ],
        out_specs=pl.BlockSpec((tm, d), lambda r: (r, 0)),
        out_shape=jax.ShapeDtypeStruct((n, d), F32),
        compiler_params=_cparams(("arbitrary",)),
        name="ffn",
    )(x2, ya2, yb2, wts["w_out_a"], wts["w_out_b"], wts["norm_ffn"], wts["w_g"], wts["w_u"], wts["w_o"])


def _block_diag(wb):
    n, a, b = wb.shape
    eye = jnp.eye(n, dtype=wb.dtype)
    return (eye[:, None, :, None] * wb[:, :, None, :]).reshape(n * a, n * b)


def _ffn_chunk(d_ff):
    for c in (d_ff, 512, 256, 128):
        if d_ff % c == 0:
            return c
    return d_ff


def _prep_weights(norm_mix, w_in, conv_w, conv_b, w_rg, b_rg, w_ig, b_ig, lru_lambda,
                  q_norm, k_norm, w_out, norm_ffn, w_ffn_in, w_ffn_out):
    d = w_in.shape[0]
    w = conv_b.shape[0]
    aw = N_HEADS * HEAD_DIM
    kw = N_KV_HEADS * HEAD_DIM
    iw = N_IDX_HEADS * IDX_DIM
    o = np.cumsum([0, w, w, aw, kw, kw, iw, IDX_DIM, N_IDX_HEADS])
    wb = w_in.astype(BF16)
    w_kw = jnp.pad(wb[:, o[6]:o[8]], ((0, 0), (0, LANES - (IDX_DIM + N_IDX_HEADS))))
    d_ff = w_ffn_out.shape[0]
    fc = _ffn_chunk(d_ff)
    nck = d_ff // fc
    wfi = w_ffn_in.astype(BF16)
    hd_id = np.arange(aw) // HEAD_DIM
    aug_id = np.arange(N_KV_HEADS * K_AUG)
    return {
        "norm_mix": norm_mix.reshape(1, d),
        "w_a": wb[:, o[0]:o[2]],
        "w_q": wb[:, o[2]:o[3]],
        "w_kv": wb[:, o[3]:o[5]],
        "w_qi": wb[:, o[5]:o[6]],
        "w_kw": w_kw,
        "conv_w": jnp.pad(conv_w, ((0, SUBLANES - CONV_W), (0, 0))),
        "conv_b": conv_b.reshape(1, w),
        "w_rg": _block_diag(w_rg).astype(BF16),
        "w_ig": _block_diag(w_ig).astype(BF16),
        "b_rg": b_rg.reshape(1, w),
        "b_ig": b_ig.reshape(1, w),
        "lam": lru_lambda.reshape(1, w),
        "q_norm": jnp.tile(q_norm, N_HEADS).reshape(1, aw),
        "k_norm": jnp.tile(k_norm, N_KV_HEADS).reshape(1, kw),
        "g": jnp.asarray(hd_id[:, None] == hd_id[None, :], BF16),
        "g_aug": jnp.asarray((aug_id[:, None] // K_AUG == aug_id[None, :] // K_AUG)
                             & (aug_id[:, None] % K_AUG < HEAD_DIM), BF16),
        "w_out_a": w_out[:w].astype(BF16),
        "w_out_b": w_out[w:].astype(BF16),
        "norm_ffn": norm_ffn.reshape(1, d),
        "w_g": wfi[:, :d_ff].reshape(d, nck, fc).transpose(1, 0, 2),
        "w_u": wfi[:, d_ff:].reshape(d, nck, fc).transpose(1, 0, 2),
        "w_o": w_ffn_out.astype(BF16).reshape(nck, fc, d),
    }


def _rope_tables(pos):
    half = HEAD_DIM // 2
    inv = ROPE_THETA ** (-jnp.arange(half, dtype=F32) / half)
    ang = pos.astype(F32)[:, None] * inv[None, :]
    cos = jnp.cos(ang)
    sin = jnp.sin(ang)
    reps = LANES // HEAD_DIM
    return (jnp.tile(jnp.concatenate([cos, cos], axis=1), (1, reps)),
            jnp.tile(jnp.concatenate([-sin, sin], axis=1), (1, reps)))


def _query_operands(q, qi, wi, tq):
    bsz, t_len, _ = q.shape
    pad_q = ((0, 0), (0, tq - t_len), (0, 0))
    grp = N_HEADS // N_KV_HEADS
    qg = jnp.pad(q, pad_q).reshape(bsz, 1, tq, N_KV_HEADS, grp, HEAD_DIM)
    qg = qg.transpose(0, 1, 3, 5, 4, 2).reshape(bsz, 1, N_KV_HEADS * HEAD_DIM, grp * tq)
    return qg, jnp.pad(qi, pad_q).transpose(0, 2, 1), jnp.pad(wi, pad_q).transpose(0, 2, 1)


def _layer(x, pos0, conv_prev, h_prev, k_past, v_past, ki_past, wts):
    bsz, t_len, d = x.shape
    w = conv_prev.shape[2]
    kw = N_KV_HEADS * HEAD_DIM
    pos = pos0 + jnp.arange(t_len, dtype=jnp.int32)
    cos, sin = _rope_tables(pos)
    sk = DSA_KEY_BLOCK
    cprev = jnp.pad(conv_prev, ((0, 0), (SUBLANES - (CONV_W - 1), 0), (0, 0)))
    hprev = h_prev.reshape(bsz, 1, w)
    if k_past is None:
        assert t_len % sk == 0 and t_len % PROJ_ROWS == 0, (t_len, sk, PROJ_ROWS)
        tq = PROJ_ROWS
        outs = _proj_call(x, cos, sin, cprev, hprev, wts, tb=PROJ_ROWS, pos0=pos0, sk=sk)
        ya, k_new, v_new, ki_new, tail, hlast, qg, qiT, wiT, k_aug, ki_b, vT_blk = outs
        s_real = t_len
    else:
        p_len = k_past.shape[1]
        tq = LANES
        assert p_len % sk == 0 and t_len <= min(sk, tq), (p_len, t_len, sk)
        outs = _proj_call(x, cos, sin, cprev, hprev, wts, tb=t_len, pos0=pos0)
        ya, k_new, v_new, ki_new, tail, hlast, q, qi, kwi = outs
        qg, qiT, wiT = _query_operands(q, qi, kwi[..., IDX_DIM:IDX_DIM + N_IDX_HEADS], tq)
        k_aug, ki_b, vT_blk = _keys_call(k_past.reshape(bsz, p_len, kw), v_past.reshape(bsz, p_len, kw),
                                         ki_past, k_new, v_new, ki_new, sk=sk)
        s_real = p_len + t_len
    topk = min(TOPK_MAX, s_real // 4)
    kmax = _kmax_call(k_aug, wts["g_aug"])
    yb = _dsa_call(qg, qiT, wiT, k_aug, vT_blk, ki_b, kmax, tq=tq, sk=sk, s_real=s_real,
                   q_pos0=pos0, n_q_real=t_len, topk=topk)[:, :t_len]

    n = bsz * t_len
    tm = 512 if n % 512 == 0 else n
    y = _ffn_call(x.reshape(n, d), ya.reshape(n, w), yb.reshape(n, N_HEADS * HEAD_DIM), wts, tm=tm)
    return (y.reshape(bsz, t_len, d),
            k_new.reshape(bsz, t_len, N_KV_HEADS, HEAD_DIM),
            v_new.reshape(bsz, t_len, N_KV_HEADS, HEAD_DIM),
            ki_new,
            hlast[:, SUBLANES - 1],
            tail[:, SUBLANES - (CONV_W - 1):])


def kernel(x_prompt, x_sample, cache_k, cache_v, cache_kidx, state_h, state_conv, norm_mix, w_in, conv_w, conv_b, w_rg, b_rg, w_ig, b_ig, lru_lambda, q_norm, k_norm, w_out, norm_ffn, w_ffn_in, w_ffn_out):
    wts = _prep_weights(norm_mix, w_in, conv_w, conv_b, w_rg, b_rg, w_ig, b_ig, lru_lambda,
                        q_norm, k_norm, w_out, norm_ffn, w_ffn_in, w_ffn_out)
    bp = x_prompt.shape[0]
    w = conv_b.shape[0]
    p_len = cache_k.shape[1]
    conv0 = jnp.zeros((bp, CONV_W - 1, w), x_prompt.dtype)
    h0 = jnp.zeros((bp, w), x_prompt.dtype)
    yp, k_p, v_p, ki_p, h_p, conv_p = _layer(x_prompt, 0, conv0, h0, None, None, None, wts)
    ys, k_s, v_s, ki_s, h_s, conv_s = _layer(x_sample, p_len, state_conv, state_h,
                                             cache_k, cache_v, cache_kidx, wts)
    return (yp, ys, k_p, v_p, ki_p, h_p, conv_p, k_s, v_s, ki_s, h_s, conv_s)
```

```python
import functools

import numpy as np
import jax
import jax.numpy as jnp
from jax import lax
from jax.experimental import pallas as pl
from jax.experimental.pallas import tpu as pltpu

F32 = jnp.float32
BF16 = jnp.bfloat16

CHUNK = 64
EPS = 1e-6
LRU_BLOCKS = 8
CONV_W = 4
LRU_C = 8.0
N_HEADS = 8
N_KV_HEADS = 2
HEAD_DIM = 64
N_IDX_HEADS = 8
IDX_DIM = 64
TOPK_MAX = 256
ROPE_THETA = 10000.0

LANES = 128
SUBLANES = 8
VMEM_LIMIT = 48 * 1024 * 1024
FLT_MAX = float(np.finfo(np.float32).max)
FLT_TINY = float(np.finfo(np.float32).tiny)
SEARCH_PASS_CAP = 512
NEG_BIG = -1e30
LOG2_E = float(np.log2(np.e))
K_AUG = LANES
M_SLACK = 1.0 + 2.0 ** -5
CNT_ROWS = 4 * SUBLANES
L_FLOOR = 2.0 ** -80
BF16_SUBLANES = 16
PROJ_ROWS = 512
DSA_KEY_BLOCK = 512
BLOCKS_PER_TRIP = 2
PASSES_PER_CHECK = 2
V_ROWS = HEAD_DIM + BF16_SUBLANES


def _cparams(sem):
    return pltpu.CompilerParams(dimension_semantics=sem, vmem_limit_bytes=VMEM_LIMIT)


def _const_spec(shape):
    nd = len(shape)
    return pl.BlockSpec(shape, lambda *_: (0,) * nd)


def _swap_halves(z):
    lane = lax.broadcasted_iota(jnp.int32, z.shape, 1)
    lo_half = (lane % HEAD_DIM) < (HEAD_DIM // 2)
    return jnp.where(lo_half, pltpu.roll(z, LANES - HEAD_DIM // 2, 1), pltpu.roll(z, HEAD_DIM // 2, 1))


def _rope(z, cos, sin):
    outs = []
    for s in range(z.shape[1] // LANES):
        zs = z[:, s * LANES:(s + 1) * LANES]
        outs.append(zs * cos + _swap_halves(zs) * sin)
    return outs[0] if len(outs) == 1 else jnp.concatenate(outs, axis=1)


def _head_rmsnorm(z, gain, g):
    z2 = z * z
    hi = z2.astype(BF16)
    lo = (z2 - hi.astype(F32)).astype(BF16)
    ss = jnp.dot(hi, g, preferred_element_type=F32) + jnp.dot(lo, g, preferred_element_type=F32)
    return z * lax.rsqrt(ss * (1.0 / HEAD_DIM) + EPS) * gain


def _gelu_tanh(x):
    return 0.5 * x * (1.0 + jnp.tanh(np.sqrt(2.0 / np.pi).astype(np.float32) * (x + 0.044715 * (x * x * x))))


def _proj_kernel(x_ref, cos_ref, sin_ref, cprev_ref, hprev_ref,
                 nmix_ref, wa_ref, wq_ref, wkv_ref, wqi_ref, wkw_ref,
                 convw_ref, convb_ref, wrg_ref, wig_ref, brg_ref, big_ref, lam_ref,
                 qn_ref, kn_ref, g_ref,
                 ya_ref, k_ref, v_ref, ki_ref, tail_ref, hlast_ref, *rest, tb, pos0, wi_scale, dsa_layout):
    cbuf, hcar = rest[-2:]
    t = pl.program_id(1)
    w = cbuf.shape[1]

    @pl.when(t == 0)
    def _():
        cbuf[0:SUBLANES, :] = cprev_ref[0]
        hcar[...] = hprev_ref[0]

    x = x_ref[0]
    ms = jnp.mean(x * x, axis=-1, keepdims=True)
    h = (x * lax.rsqrt(ms + EPS) * nmix_ref[...]).astype(BF16)

    pa = jnp.dot(h, wa_ref[...], preferred_element_type=F32)
    xr = pa[:, :w]
    gate = pa[:, w:]
    cbuf[SUBLANES:SUBLANES + tb, :] = xr
    xc = convb_ref[...]
    for j in range(CONV_W):
        off = SUBLANES - (CONV_W - 1) + j
        xc = xc + cbuf[off:off + tb, :] * convw_ref[j:j + 1, :]
    tail = cbuf[tb:tb + SUBLANES, :]
    cbuf[0:SUBLANES, :] = tail
    tail_ref[0] = tail

    xcb = xc.astype(BF16)
    r = jax.nn.sigmoid(jnp.dot(xcb, wrg_ref[...], preferred_element_type=F32) + brg_ref[...])
    ig = jax.nn.sigmoid(jnp.dot(xcb, wig_ref[...], preferred_element_type=F32) + big_ref[...])
    nl = -lam_ref[...]
    softplus = jnp.maximum(nl, 0.0) + jnp.log(1.0 + jnp.exp(-jnp.abs(nl)))
    log_a = (-LRU_C) * r * softplus
    a = jnp.exp(log_a)
    mult = jnp.sqrt(1.0 - jnp.exp(2.0 * log_a))
    row = lax.broadcasted_iota(jnp.int32, (tb, 1), 0)
    mult = jnp.where(row + (pos0 + t * tb) == 0, 1.0, mult)
    b = mult * ig * xc
    s = 1
    while s < SUBLANES:
        keep = (row % SUBLANES) >= s
        a_s = jnp.where(keep, pltpu.roll(a, s, 0), 1.0)
        b_s = jnp.where(keep, pltpu.roll(b, s, 0), 0.0)
        b = a * b_s + b
        a = a * a_s
        s *= 2
    h_in = hcar[...]
    groups = []
    for gi in range(tb // SUBLANES):
        rows = slice(gi * SUBLANES, (gi + 1) * SUBLANES)
        hg = a[rows, :] * h_in + b[rows, :]
        groups.append(hg)
        h_in = hg[SUBLANES - 1:SUBLANES, :]
    hs = jnp.concatenate(groups, axis=0)
    hcar[...] = hs[tb - 1:tb, :]
    hlast_ref[0] = hs[tb - SUBLANES:tb, :]
    ya_ref[0] = (hs * _gelu_tanh(gate)).astype(ya_ref.dtype)

    cos = cos_ref[...]
    sin = sin_ref[...]
    g = g_ref[...]
    pq = jnp.dot(h, wq_ref[...], preferred_element_type=F32)
    q = _rope(_head_rmsnorm(pq, qn_ref[...], g), cos, sin) * (LOG2_E * HEAD_DIM ** -0.5)

    pkv = jnp.dot(h, wkv_ref[...], preferred_element_type=F32)
    kw_ = N_KV_HEADS * HEAD_DIM
    k = _rope(_head_rmsnorm(pkv[:, :kw_], kn_ref[...], g[:kw_, :kw_]), cos, sin)
    v = pkv[:, kw_:]
    k_ref[0] = k
    v_ref[0] = v

    pqi = jnp.dot(h, wqi_ref[...], preferred_element_type=F32)
    qi = _rope(pqi, cos, sin)

    pkw = jnp.dot(h, wkw_ref[...], preferred_element_type=F32)
    lane = lax.broadcasted_iota(jnp.int32, pkw.shape, 1)
    kwv = jnp.where(lane < IDX_DIM, _rope(pkw, cos, sin), pkw * wi_scale)
    ki_ref[0] = kwv[:, :IDX_DIM]

    if not dsa_layout:
        q_ref, qi_ref, kw_ref = rest[:3]
        q_ref[0] = q.astype(q_ref.dtype)
        qi_ref[0] = qi.astype(qi_ref.dtype)
        kw_ref[0] = kwv
        return

    qg_ref, qiT_ref, wiT_ref, kaug_ref, kib_ref, vT_ref = rest[:6]
    grp = N_HEADS // N_KV_HEADS
    qT = q.T.astype(qg_ref.dtype)
    qg_ref[0, 0] = jnp.concatenate(
        [jnp.concatenate([qT[(gk * grp + hq) * HEAD_DIM:(gk * grp + hq + 1) * HEAD_DIM, :]
                          for hq in range(grp)], axis=1) for gk in range(N_KV_HEADS)], axis=0)
    qiT_ref[0] = qi.T.astype(qiT_ref.dtype)
    wiT_ref[0] = kwv.T[IDX_DIM:IDX_DIM + N_IDX_HEADS, :]
    kib_ref[0] = kwv[:, :IDX_DIM].astype(kib_ref.dtype)
    one_lane = jnp.where(lax.broadcasted_iota(jnp.int32, (tb, K_AUG - HEAD_DIM), 1) == 0, 1.0, 0.0)
    kaug_ref[0] = jnp.concatenate(
        [piece for gk in range(N_KV_HEADS)
         for piece in (k[:, gk * HEAD_DIM:(gk + 1) * HEAD_DIM], one_lane)], axis=1).astype(kaug_ref.dtype)
    vT = v.T
    one_row = jnp.where(lax.broadcasted_iota(jnp.int32, (V_ROWS - HEAD_DIM, tb), 0) == 0, 1.0, 0.0)
    vT_ref[0, 0] = jnp.concatenate(
        [piece for gk in range(N_KV_HEADS)
         for piece in (vT[gk * HEAD_DIM:(gk + 1) * HEAD_DIM, :], one_row)], axis=0).astype(vT_ref.dtype)


def _proj_call(x, cos, sin, cprev, hprev, wts, *, tb, pos0, sk=None):
    bsz, t_len, d = x.shape
    w = wts["conv_b"].shape[1]
    aw = N_HEADS * HEAD_DIM
    kvw = 2 * N_KV_HEADS * HEAD_DIM
    nt = t_len // tb
    wi_scale = float(N_IDX_HEADS ** -0.5 * IDX_DIM ** -0.5)
    names = ("norm_mix", "w_a", "w_q", "w_kv", "w_qi", "w_kw", "conv_w", "conv_b",
             "w_rg", "w_ig", "b_rg", "b_ig", "lam", "q_norm", "k_norm", "g")
    warrs = [wts[n] for n in names]
    in_specs = [
        pl.BlockSpec((1, tb, d), lambda b, t: (b, t, 0)),
        pl.BlockSpec((tb, LANES), lambda b, t: (t, 0)),
        pl.BlockSpec((tb, LANES), lambda b, t: (t, 0)),
        pl.BlockSpec((1, SUBLANES, w), lambda b, t: (b, 0, 0)),
        pl.BlockSpec((1, 1, w), lambda b, t: (b, 0, 0)),
    ] + [_const_spec(a.shape) for a in warrs]
    out_shape = [
        jax.ShapeDtypeStruct((bsz, t_len, w), BF16),
        jax.ShapeDtypeStruct((bsz, t_len, kvw // 2), F32),
        jax.ShapeDtypeStruct((bsz, t_len, kvw // 2), F32),
        jax.ShapeDtypeStruct((bsz, t_len, IDX_DIM), F32),
        jax.ShapeDtypeStruct((bsz, SUBLANES, w), F32),
        jax.ShapeDtypeStruct((bsz, SUBLANES, w), F32),
    ]
    out_specs = [
        pl.BlockSpec((1, tb, w), lambda b, t: (b, t, 0)),
        pl.BlockSpec((1, tb, kvw // 2), lambda b, t: (b, t, 0)),
        pl.BlockSpec((1, tb, kvw // 2), lambda b, t: (b, t, 0)),
        pl.BlockSpec((1, tb, IDX_DIM), lambda b, t: (b, t, 0)),
        pl.BlockSpec((1, SUBLANES, w), lambda b, t: (b, 0, 0)),
        pl.BlockSpec((1, SUBLANES, w), lambda b, t: (b, 0, 0)),
    ]
    dsa_layout = sk is not None
    if dsa_layout:
        grp = N_HEADS // N_KV_HEADS
        per_sk = sk // tb
        out_shape += [
            jax.ShapeDtypeStruct((bsz, nt, kvw // 2, grp * tb), BF16),
            jax.ShapeDtypeStruct((bsz, aw, t_len), BF16),
            jax.ShapeDtypeStruct((bsz, N_IDX_HEADS, t_len), F32),
            jax.ShapeDtypeStruct((bsz, t_len, N_KV_HEADS * K_AUG), BF16),
            jax.ShapeDtypeStruct((bsz, t_len, IDX_DIM), BF16),
            jax.ShapeDtypeStruct((bsz, t_len // sk, N_KV_HEADS * V_ROWS, sk), BF16),
        ]
        out_specs += [
            pl.BlockSpec((1, 1, kvw // 2, grp * tb), lambda b, t: (b, t, 0, 0)),
            pl.BlockSpec((1, aw, tb), lambda b, t: (b, 0, t)),
            pl.BlockSpec((1, N_IDX_HEADS, tb), lambda b, t: (b, 0, t)),
            pl.BlockSpec((1, tb, N_KV_HEADS * K_AUG), lambda b, t: (b, t, 0)),
            pl.BlockSpec((1, tb, IDX_DIM), lambda b, t: (b, t, 0)),
            pl.BlockSpec((1, 1, N_KV_HEADS * V_ROWS, tb), lambda b, t: (b, t // per_sk, 0, t % per_sk)),
        ]
    else:
        out_shape += [jax.ShapeDtypeStruct((bsz, t_len, aw), BF16),
                      jax.ShapeDtypeStruct((bsz, t_len, aw), BF16),
                      jax.ShapeDtypeStruct((bsz, t_len, LANES), F32)]
        out_specs += [pl.BlockSpec((1, tb, aw), lambda b, t: (b, t, 0)),
                      pl.BlockSpec((1, tb, aw), lambda b, t: (b, t, 0)),
                      pl.BlockSpec((1, tb, LANES), lambda b, t: (b, t, 0))]
    return pl.pallas_call(
        functools.partial(_proj_kernel, tb=tb, pos0=pos0, wi_scale=wi_scale, dsa_layout=dsa_layout),
        grid=(bsz, nt),
        in_specs=in_specs,
        out_specs=out_specs,
        out_shape=out_shape,
        scratch_shapes=[pltpu.VMEM((SUBLANES + tb, w), F32), pltpu.VMEM((1, w), F32)],
        compiler_params=_cparams(("arbitrary", "arbitrary")),
        name="proj",
    )(x, cos, sin, cprev, hprev, *warrs)


def _for_blocks(n, body, init):
    def trip(jj, carry):
        for u in range(BLOCKS_PER_TRIP):
            carry = body(BLOCKS_PER_TRIP * jj + u, carry)
        return carry
    whole = n // BLOCKS_PER_TRIP
    carry = lax.fori_loop(0, whole, trip, init)
    return lax.fori_loop(whole * BLOCKS_PER_TRIP, n, body, carry)


def _dsa_kernel(qT_ref, qiT_ref, wiT_ref, k_ref, vT_ref, ki_ref, kmax_ref, o_ref,
                sc_scr, m_scr, acc_scr, qa_scr, *, tq, sk, s_real, n_kblocks, q_pos0, n_q_real, topk):
    i = pl.program_id(1)
    q_idx = i * tq + lax.broadcasted_iota(jnp.int32, (1, tq), 1)
    q_adm_end = ((q_pos0 + q_idx) // CHUNK + 1) * CHUNK
    q_adm_end = jnp.minimum(q_adm_end, s_real)
    last_q = jnp.minimum(i * tq + tq - 1, n_q_real - 1)
    blk_end = jnp.minimum(((q_pos0 + last_q) // CHUNK + 1) * CHUNK, s_real)
    nkb = jnp.minimum((blk_end + sk - 1) // sk, n_kblocks)

    def score_body(j, carry):
        smax, smin = carry
        off = pl.multiple_of(j * sk, sk)
        kib = ki_ref[0, pl.ds(off, sk), :]
        acc = jnp.zeros((sk, tq), F32)
        for hh in range(N_IDX_HEADS):
            sc = jnp.dot(kib, qiT_ref[0, hh * IDX_DIM:(hh + 1) * IDX_DIM, :], preferred_element_type=F32)
            acc = acc + jnp.maximum(sc, 0.0) * wiT_ref[0, hh:hh + 1, :]
        kidx = off + lax.broadcasted_iota(jnp.int32, (sk, 1), 0)
        adm = kidx < q_adm_end
        hi_part = jnp.where(adm, acc, -jnp.inf)
        lo_part = jnp.where(adm, acc, jnp.inf)
        sc_scr[pl.ds(off, sk), :] = hi_part
        return (jnp.maximum(smax, jnp.max(hi_part.reshape(sk // CNT_ROWS, CNT_ROWS, tq), axis=0)),
                jnp.minimum(smin, jnp.min(lo_part.reshape(sk // CNT_ROWS, CNT_ROWS, tq), axis=0)))

    smax, smin = _for_blocks(
        nkb, score_body,
        (jnp.full((CNT_ROWS, tq), -jnp.inf, F32), jnp.full((CNT_ROWS, tq), jnp.inf, F32)))
    smax = jnp.max(smax, axis=0, keepdims=True)
    smin = jnp.min(smin, axis=0, keepdims=True)

    def count(thr, strict=False):
        def body(j, cnt):
            off = pl.multiple_of(j * sk, sk)
            blk = sc_scr[pl.ds(off, sk), :]
            above = (blk > thr if strict else blk >= thr).astype(jnp.int32)
            return cnt + jnp.sum(above.reshape(sk // CNT_ROWS, CNT_ROWS, tq), axis=0)
        cnt = _for_blocks(nkb, body, jnp.zeros((CNT_ROWS, tq), jnp.int32))
        return jnp.sum(cnt, axis=0, keepdims=True)

    open0 = jnp.logical_and(q_adm_end > topk, q_idx < n_q_real).astype(jnp.int32)

    zero = jnp.zeros((1, tq), F32)
    c_nonneg = count(zero)
    c_pos = count(zero, strict=True)
    is_open0 = open0 > 0
    at_zero = jnp.logical_and(is_open0, jnp.logical_or(
        c_nonneg == topk, jnp.logical_and(c_pos < topk, c_nonneg > topk)))
    tie0 = jnp.logical_and(at_zero, c_nonneg > topk).astype(jnp.int32)
    thr0 = jnp.where(at_zero, 0.0, -FLT_MAX)
    open_a = jnp.where(at_zero, 0, open0)
    above_zero = c_pos >= topk
    lo0 = jnp.where(above_zero, 0.0, smin)
    hi0 = jnp.where(above_zero, smax * (1.0 + 2.0 ** -22) + FLT_TINY, 0.0)

    def search_cond(st):
        it, _, _, _, _, _, n_open = st
        return jnp.logical_and(it < SEARCH_PASS_CAP, n_open > 0)

    def search_step(st):
        lo, hi, thr, opn, tie = st
        mid = 0.5 * lo + 0.5 * hi
        cnt = count(mid)
        is_open = opn > 0
        hit = jnp.logical_and(is_open, cnt == topk)
        stuck = jnp.logical_and(jnp.logical_and(is_open, cnt != topk),
                                jnp.logical_or(mid == lo, mid == hi))
        thr = jnp.where(hit, mid, jnp.where(stuck, lo, thr))
        tie = jnp.where(stuck, 1, tie)
        moving = jnp.logical_and(is_open, jnp.logical_not(jnp.logical_or(hit, stuck)))
        lo = jnp.where(jnp.logical_and(moving, cnt > topk), mid, lo)
        hi = jnp.where(jnp.logical_and(moving, cnt < topk), mid, hi)
        opn = jnp.where(jnp.logical_or(hit, stuck), 0, opn)
        return lo, hi, thr, opn, tie

    def search_body(st):
        it, rest = st[0], st[1:-1]
        for _ in range(PASSES_PER_CHECK):
            rest = search_step(rest)
        return (it + PASSES_PER_CHECK, *rest, jnp.sum(rest[3]))

    _, _, _, thr, _, tie, _ = lax.while_loop(
        search_cond, search_body, (jnp.int32(0), lo0, hi0, thr0, open_a, tie0, jnp.sum(open_a)))

    @pl.when(jnp.sum(tie) > 0)
    def _():
        cnt_gt = count(thr, strict=True)
        need = jnp.where(tie > 0, topk - cnt_gt, jnp.int32(2 ** 30)).astype(F32)
        half = sk // 2
        r_i = lax.broadcasted_iota(jnp.int32, (half, half), 0)
        c_i = lax.broadcasted_iota(jnp.int32, (half, half), 1)
        tri = (c_i <= r_i).astype(BF16)

        def tie_body(j, seen):
            off = pl.multiple_of(j * sk, sk)
            blk = sc_scr[pl.ds(off, sk), :]
            eq = blk == thr
            eqb = eq.astype(BF16)
            r_top = jnp.dot(tri, eqb[:half], preferred_element_type=F32) + seen
            r_bot = jnp.dot(tri, eqb[half:], preferred_element_type=F32) + r_top[half - 1:half, :]
            rank = jnp.concatenate([r_top, r_bot], axis=0)
            sc_scr[pl.ds(off, sk), :] = jnp.where(jnp.logical_and(eq, rank > need), -jnp.inf, blk)
            return r_bot[half - 1:half, :]

        _for_blocks(nkb, tie_body, jnp.zeros((1, tq), F32))

    grp = N_HEADS // N_KV_HEADS

    def finish():
        outs = []
        for gk in range(N_KV_HEADS):
            acc = acc_scr[gk]
            for hq in range(grp):
                sl = slice(hq * tq, (hq + 1) * tq)
                outs.append(acc[:HEAD_DIM, sl] / acc[HEAD_DIM:HEAD_DIM + 1, sl])
        o_ref[0] = jnp.concatenate(outs, axis=0).T.astype(o_ref.dtype)

    for gk in range(N_KV_HEADS):
        qb = qT_ref[0, 0, gk * HEAD_DIM:(gk + 1) * HEAD_DIM, :]
        qf = qb.astype(F32)
        qn2 = jnp.sum(qf * qf, axis=0, keepdims=True)
        m = jnp.sqrt(qn2 * kmax_ref[0, :, gk * K_AUG:gk * K_AUG + 1]) * M_SLACK
        row = lax.broadcasted_iota(jnp.int32, (BF16_SUBLANES, grp * tq), 0)
        qa_scr[gk, 0:HEAD_DIM, :] = qb
        qa_scr[gk, HEAD_DIM:HEAD_DIM + BF16_SUBLANES, :] = jnp.where(row == 0, -m, 0.0).astype(BF16)
        qa_scr[gk, HEAD_DIM + BF16_SUBLANES:, :] = jnp.zeros(
            (K_AUG - HEAD_DIM - BF16_SUBLANES, grp * tq), BF16)
    acc_scr[...] = jnp.zeros(acc_scr.shape, F32)

    def fast_body(j, carry):
        off = pl.multiple_of(j * sk, sk)
        sel = sc_scr[pl.ds(off, sk), :] >= thr
        logits = [jnp.dot(k_ref[0, pl.ds(off, sk), gk * K_AUG:(gk + 1) * K_AUG], qa_scr[gk],
                          preferred_element_type=F32) for gk in range(N_KV_HEADS)]
        for gk in range(N_KV_HEADS):
            s = logits[gk]
            p = jnp.concatenate(
                [jnp.exp2(jnp.where(sel, s[:, hq * tq:(hq + 1) * tq], NEG_BIG)) for hq in range(grp)],
                axis=1).astype(BF16)
            vtb = vT_ref[0, j, gk * V_ROWS:(gk + 1) * V_ROWS, :]
            acc_scr[gk] += jnp.dot(vtb, p, preferred_element_type=F32)
        return carry

    _for_blocks(nkb, fast_body, 0)
    l_min = jnp.min(acc_scr[:, HEAD_DIM:HEAD_DIM + 1, :])
    healthy = l_min > L_FLOOR

    @pl.when(healthy)
    def _():
        finish()

    @pl.when(jnp.logical_not(healthy))
    def _():
        m_scr[...] = jnp.full(m_scr.shape, NEG_BIG, F32)
        acc_scr[...] = jnp.zeros(acc_scr.shape, F32)

        def slow_body(j, carry):
            off = pl.multiple_of(j * sk, sk)
            sel = sc_scr[pl.ds(off, sk), :] >= thr
            for gk in range(N_KV_HEADS):
                kb = k_ref[0, pl.ds(off, sk), gk * K_AUG:gk * K_AUG + HEAD_DIM]
                vtb = vT_ref[0, j, gk * V_ROWS:(gk + 1) * V_ROWS, :]
                s = jnp.dot(kb, qT_ref[0, 0, gk * HEAD_DIM:(gk + 1) * HEAD_DIM, :],
                            preferred_element_type=F32)
                s = jnp.concatenate(
                    [jnp.where(sel, s[:, hq * tq:(hq + 1) * tq], NEG_BIG) for hq in range(grp)], axis=1)
                m_old = m_scr[gk]
                m_new = jnp.maximum(m_old, jnp.max(s, axis=0, keepdims=True))
                p = jnp.exp2(s - m_new).astype(BF16)
                alpha = jnp.exp2(m_old - m_new)
                acc_scr[gk] = alpha * acc_scr[gk] + jnp.dot(vtb, p, preferred_element_type=F32)
                m_scr[gk] = m_new
            return carry

        lax.fori_loop(0, nkb, slow_body, 0)
        finish()


def _kmax_kernel(k_ref, g_ref, o_ref):
    kf = k_ref[0].astype(F32)
    ss = jnp.dot((kf * kf).astype(BF16), g_ref[...], preferred_element_type=F32)
    o_ref[0] = jnp.max(ss, axis=0, keepdims=True)


def _kmax_call(k_b, g):
    bsz, s_pad, kw = k_b.shape
    return pl.pallas_call(
        _kmax_kernel,
        grid=(bsz,),
        in_specs=[pl.BlockSpec((1, s_pad, kw), lambda b: (b, 0, 0)), _const_spec(g.shape)],
        out_specs=pl.BlockSpec((1, 1, kw), lambda b: (b, 0, 0)),
        out_shape=jax.ShapeDtypeStruct((bsz, 1, kw), F32),
        compiler_params=_cparams(("arbitrary",)),
        name="kmax",
    )(k_b, g)


def _keys_kernel(ck_ref, cv_ref, cki_ref, nk_ref, nv_ref, nki_ref, kaug_ref, kib_ref, vT_ref, *, n_cache_blocks):
    j = pl.program_id(1)
    sk = ck_ref.shape[1]
    t_new = nk_ref.shape[1]

    def cached_or_new(c_ref, n_ref):
        new = jnp.concatenate([n_ref[0], jnp.zeros((sk - t_new, n_ref.shape[2]), F32)], axis=0)
        return jnp.where(j < n_cache_blocks, c_ref[0], new)

    k = cached_or_new(ck_ref, nk_ref)
    v = cached_or_new(cv_ref, nv_ref)
    kib_ref[0] = cached_or_new(cki_ref, nki_ref).astype(kib_ref.dtype)
    one_lane = jnp.where(lax.broadcasted_iota(jnp.int32, (sk, K_AUG - HEAD_DIM), 1) == 0, 1.0, 0.0)
    kaug_ref[0] = jnp.concatenate(
        [piece for gk in range(N_KV_HEADS)
         for piece in (k[:, gk * HEAD_DIM:(gk + 1) * HEAD_DIM], one_lane)], axis=1).astype(kaug_ref.dtype)
    vT = v.T
    one_row = jnp.where(lax.broadcasted_iota(jnp.int32, (V_ROWS - HEAD_DIM, sk), 0) == 0, 1.0, 0.0)
    vT_ref[0, 0] = jnp.concatenate(
        [piece for gk in range(N_KV_HEADS)
         for piece in (vT[gk * HEAD_DIM:(gk + 1) * HEAD_DIM, :], one_row)], axis=0).astype(vT_ref.dtype)


def _keys_call(cache_k, cache_v, cache_ki, k_new, v_new, ki_new, *, sk):
    bsz, p_len, kw = cache_k.shape
    t_new = k_new.shape[1]
    ncb = p_len // sk
    nkb = ncb + 1

    def cache_spec(width):
        return pl.BlockSpec((1, sk, width), lambda b, j: (b, jnp.minimum(j, ncb - 1), 0))

    def new_spec(width):
        return pl.BlockSpec((1, t_new, width), lambda b, j: (b, 0, 0))

    return pl.pallas_call(
        functools.partial(_keys_kernel, n_cache_blocks=ncb),
        grid=(bsz, nkb),
        in_specs=[cache_spec(kw), cache_spec(kw), cache_spec(IDX_DIM),
                  new_spec(kw), new_spec(kw), new_spec(IDX_DIM)],
        out_specs=(
            pl.BlockSpec((1, sk, N_KV_HEADS * K_AUG), lambda b, j: (b, j, 0)),
            pl.BlockSpec((1, sk, IDX_DIM), lambda b, j: (b, j, 0)),
            pl.BlockSpec((1, 1, N_KV_HEADS * V_ROWS, sk), lambda b, j: (b, j, 0, 0)),
        ),
        out_shape=(
            jax.ShapeDtypeStruct((bsz, nkb * sk, N_KV_HEADS * K_AUG), BF16),
            jax.ShapeDtypeStruct((bsz, nkb * sk, IDX_DIM), BF16),
            jax.ShapeDtypeStruct((bsz, nkb, N_KV_HEADS * V_ROWS, sk), BF16),
        ),
        compiler_params=_cparams(("arbitrary", "arbitrary")),
        name="keys",
    )(cache_k, cache_v, cache_ki, k_new, v_new, ki_new)


def _dsa_call(qg, qiT, wiT, k_aug, vT_blk, ki_b, kmax, *, tq, sk, s_real, q_pos0, n_q_real, topk):
    bsz, aw, tq_total = qiT.shape
    n_kblocks = vT_blk.shape[1]
    s_pad = k_aug.shape[1]
    nq = tq_total // tq
    grp = N_HEADS // N_KV_HEADS
    kern = functools.partial(_dsa_kernel, tq=tq, sk=sk, s_real=s_real, n_kblocks=n_kblocks,
                             q_pos0=q_pos0, n_q_real=n_q_real, topk=topk)
    return pl.pallas_call(
        kern,
        grid=(bsz, nq),
        in_specs=[
            pl.BlockSpec((1, 1, N_KV_HEADS * HEAD_DIM, grp * tq), lambda b, i: (b, i, 0, 0)),
            pl.BlockSpec((1, aw, tq), lambda b, i: (b, 0, i)),
            pl.BlockSpec((1, N_IDX_HEADS, tq), lambda b, i: (b, 0, i)),
            pl.BlockSpec((1, s_pad, N_KV_HEADS * K_AUG), lambda b, i: (b, 0, 0)),
            pl.BlockSpec((1, n_kblocks, N_KV_HEADS * V_ROWS, sk), lambda b, i: (b, 0, 0, 0)),
            pl.BlockSpec((1, s_pad, IDX_DIM), lambda b, i: (b, 0, 0)),
            pl.BlockSpec((1, 1, N_KV_HEADS * K_AUG), lambda b, i: (b, 0, 0)),
        ],
        out_specs=pl.BlockSpec((1, tq, aw), lambda b, i: (b, i, 0)),
        out_shape=jax.ShapeDtypeStruct((bsz, tq_total, aw), BF16),
        scratch_shapes=[
            pltpu.VMEM((s_pad, tq), F32),
            pltpu.VMEM((N_KV_HEADS, 1, grp * tq), F32),
            pltpu.VMEM((N_KV_HEADS, V_ROWS, grp * tq), F32),
            pltpu.VMEM((N_KV_HEADS, K_AUG, grp * tq), BF16),
        ],
        compiler_params=_cparams(("arbitrary", "arbitrary")),
        name="dsa",
    )(qg, qiT, wiT, k_aug, vT_blk, ki_b, kmax)


def _ffn_kernel(x_ref, ya_ref, yb_ref, wout_ref, nf_ref, wgu_ref, wo_ref, o_ref):
    w = ya_ref.shape[1]
    d_ff = wo_ref.shape[0]
    x1 = x_ref[...] + jnp.dot(ya_ref[...], wout_ref[:w, :], preferred_element_type=F32)
    x1 = x1 + jnp.dot(yb_ref[...], wout_ref[w:, :], preferred_element_type=F32)
    ms = jnp.mean(x1 * x1, axis=-1, keepdims=True)
    hf = (x1 * lax.rsqrt(ms + EPS) * nf_ref[...]).astype(BF16)
    gt = jnp.dot(hf, wgu_ref[:, :d_ff], preferred_element_type=F32)
    up = jnp.dot(hf, wgu_ref[:, d_ff:], preferred_element_type=F32)
    act = (gt * jax.nn.sigmoid(gt) * up).astype(BF16)
    o_ref[...] = x1 + jnp.dot(act, wo_ref[...], preferred_element_type=F32)


def _ffn_call(x2, ya2, yb2, wts, *, tm):
    n, d = x2.shape
    w = ya2.shape[1]
    aw = yb2.shape[1]
    single = pl.Buffered(1)

    def wspec(shape):
        nd = len(shape)
        return pl.BlockSpec(shape, lambda *_: (0,) * nd, pipeline_mode=single)

    names = ("w_out", "norm_ffn", "w_gu", "w_o")
    return pl.pallas_call(
        _ffn_kernel,
        grid=(n // tm,),
        in_specs=[
            pl.BlockSpec((tm, d), lambda r: (r, 0)),
            pl.BlockSpec((tm, w), lambda r: (r, 0)),
            pl.BlockSpec((tm, aw), lambda r: (r, 0)),
        ] + [wspec(wts[k].shape) for k in names],
        out_specs=pl.BlockSpec((tm, d), lambda r: (r, 0)),
        out_shape=jax.ShapeDtypeStruct((n, d), F32),
        compiler_params=_cparams(("arbitrary",)),
        name="ffn",
    )(x2, ya2, yb2, *[wts[k] for k in names])


def _block_diag(wb):
    n, a, b = wb.shape
    eye = jnp.eye(n, dtype=wb.dtype)
    return (eye[:, None, :, None] * wb[:, :, None, :]).reshape(n * a, n * b)


def _prep_weights(norm_mix, w_in, conv_w, conv_b, w_rg, b_rg, w_ig, b_ig, lru_lambda,
                  q_norm, k_norm, w_out, norm_ffn, w_ffn_in, w_ffn_out):
    d = w_in.shape[0]
    w = conv_b.shape[0]
    aw = N_HEADS * HEAD_DIM
    kw = N_KV_HEADS * HEAD_DIM
    iw = N_IDX_HEADS * IDX_DIM
    o = np.cumsum([0, w, w, aw, kw, kw, iw, IDX_DIM, N_IDX_HEADS])
    wb = w_in.astype(BF16)
    w_kw = jnp.pad(wb[:, o[6]:o[8]], ((0, 0), (0, LANES - (IDX_DIM + N_IDX_HEADS))))
    hd_id = np.arange(aw) // HEAD_DIM
    aug_id = np.arange(N_KV_HEADS * K_AUG)
    return {
        "norm_mix": norm_mix.reshape(1, d),
        "w_a": wb[:, o[0]:o[2]],
        "w_q": wb[:, o[2]:o[3]],
        "w_kv": wb[:, o[3]:o[5]],
        "w_qi": wb[:, o[5]:o[6]],
        "w_kw": w_kw,
        "conv_w": jnp.pad(conv_w, ((0, SUBLANES - CONV_W), (0, 0))),
        "conv_b": conv_b.reshape(1, w),
        "w_rg": _block_diag(w_rg).astype(BF16),
        "w_ig": _block_diag(w_ig).astype(BF16),
        "b_rg": b_rg.reshape(1, w),
        "b_ig": b_ig.reshape(1, w),
        "lam": lru_lambda.reshape(1, w),
        "q_norm": jnp.tile(q_norm, N_HEADS).reshape(1, aw),
        "k_norm": jnp.tile(k_norm, N_KV_HEADS).reshape(1, kw),
        "g": jnp.asarray(hd_id[:, None] == hd_id[None, :], BF16),
        "g_aug": jnp.asarray((aug_id[:, None] // K_AUG == aug_id[None, :] // K_AUG)
                             & (aug_id[:, None] % K_AUG < HEAD_DIM), BF16),
        "w_out": w_out.astype(BF16),
        "norm_ffn": norm_ffn.reshape(1, d),
        "w_gu": w_ffn_in.astype(BF16),
        "w_o": w_ffn_out.astype(BF16),
    }


def _rope_tables(pos):
    half = HEAD_DIM // 2
    inv = ROPE_THETA ** (-jnp.arange(half, dtype=F32) / half)
    ang = pos.astype(F32)[:, None] * inv[None, :]
    cos = jnp.cos(ang)
    sin = jnp.sin(ang)
    reps = LANES // HEAD_DIM
    return (jnp.tile(jnp.concatenate([cos, cos], axis=1), (1, reps)),
            jnp.tile(jnp.concatenate([-sin, sin], axis=1), (1, reps)))


def _query_operands(q, qi, wi, tq):
    bsz, t_len, _ = q.shape
    pad_q = ((0, 0), (0, tq - t_len), (0, 0))
    grp = N_HEADS // N_KV_HEADS
    qg = jnp.pad(q, pad_q).reshape(bsz, 1, tq, N_KV_HEADS, grp, HEAD_DIM)
    qg = qg.transpose(0, 1, 3, 5, 4, 2).reshape(bsz, 1, N_KV_HEADS * HEAD_DIM, grp * tq)
    return qg, jnp.pad(qi, pad_q).transpose(0, 2, 1), jnp.pad(wi, pad_q).transpose(0, 2, 1)


def _layer(x, pos0, conv_prev, h_prev, k_past, v_past, ki_past, wts):
    bsz, t_len, d = x.shape
    w = conv_prev.shape[2]
    kw = N_KV_HEADS * HEAD_DIM
    pos = pos0 + jnp.arange(t_len, dtype=jnp.int32)
    cos, sin = _rope_tables(pos)
    sk = DSA_KEY_BLOCK
    cprev = jnp.pad(conv_prev, ((0, 0), (SUBLANES - (CONV_W - 1), 0), (0, 0)))
    hprev = h_prev.reshape(bsz, 1, w)
    if k_past is None:
        assert t_len % sk == 0 and t_len % PROJ_ROWS == 0, (t_len, sk, PROJ_ROWS)
        tq = PROJ_ROWS
        outs = _proj_call(x, cos, sin, cprev, hprev, wts, tb=PROJ_ROWS, pos0=pos0, sk=sk)
        ya, k_new, v_new, ki_new, tail, hlast, qg, qiT, wiT, k_aug, ki_b, vT_blk = outs
        s_real = t_len
    else:
        p_len = k_past.shape[1]
        tq = LANES
        assert p_len % sk == 0 and t_len <= min(sk, tq), (p_len, t_len, sk)
        outs = _proj_call(x, cos, sin, cprev, hprev, wts, tb=t_len, pos0=pos0)
        ya, k_new, v_new, ki_new, tail, hlast, q, qi, kwi = outs
        qg, qiT, wiT = _query_operands(q, qi, kwi[..., IDX_DIM:IDX_DIM + N_IDX_HEADS], tq)
        k_aug, ki_b, vT_blk = _keys_call(k_past.reshape(bsz, p_len, kw), v_past.reshape(bsz, p_len, kw),
                                         ki_past, k_new, v_new, ki_new, sk=sk)
        s_real = p_len + t_len
    topk = min(TOPK_MAX, s_real // 4)
    kmax = _kmax_call(k_aug, wts["g_aug"])
    yb = _dsa_call(qg, qiT, wiT, k_aug, vT_blk, ki_b, kmax, tq=tq, sk=sk, s_real=s_real,
                   q_pos0=pos0, n_q_real=t_len, topk=topk)[:, :t_len]

    n = bsz * t_len
    tm = 512 if n % 512 == 0 else n
    y = _ffn_call(x.reshape(n, d), ya.reshape(n, w), yb.reshape(n, N_HEADS * HEAD_DIM), wts, tm=tm)
    return (y.reshape(bsz, t_len, d),
            k_new.reshape(bsz, t_len, N_KV_HEADS, HEAD_DIM),
            v_new.reshape(bsz, t_len, N_KV_HEADS, HEAD_DIM),
            ki_new,
            hlast[:, SUBLANES - 1],
            tail[:, SUBLANES - (CONV_W - 1):])


def kernel(x_prompt, x_sample, cache_k, cache_v, cache_kidx, state_h, state_conv, norm_mix, w_in, conv_w, conv_b, w_rg, b_rg, w_ig, b_ig, lru_lambda, q_norm, k_norm, w_out, norm_ffn, w_ffn_in, w_ffn_out):
    wts = _prep_weights(norm_mix, w_in, conv_w, conv_b, w_rg, b_rg, w_ig, b_ig, lru_lambda,
                        q_norm, k_norm, w_out, norm_ffn, w_ffn_in, w_ffn_out)
    bp = x_prompt.shape[0]
    w = conv_b.shape[0]
    p_len = cache_k.shape[1]
    conv0 = jnp.zeros((bp, CONV_W - 1, w), x_prompt.dtype)
    h0 = jnp.zeros((bp, w), x_prompt.dtype)
    yp, k_p, v_p, ki_p, h_p, conv_p = _layer(x_prompt, 0, conv0, h0, None, None, None, wts)
    ys, k_s, v_s, ki_s, h_s, conv_s = _layer(x_sample, p_len, state_conv, state_h,
                                             cache_k, cache_v, cache_kidx, wts)
    return (yp, ys, k_p, v_p, ki_p, h_p, conv_p, k_s, v_s, ki_s, h_s, conv_s)
```

```python
import functools

import numpy as np
import jax
import jax.numpy as jnp
from jax import lax
from jax.experimental import pallas as pl
from jax.experimental.pallas import tpu as pltpu

F32 = jnp.float32
BF16 = jnp.bfloat16

CHUNK = 64
EPS = 1e-6
LRU_BLOCKS = 8
CONV_W = 4
LRU_C = 8.0
N_HEADS = 8
N_KV_HEADS = 2
HEAD_DIM = 64
N_IDX_HEADS = 8
IDX_DIM = 64
TOPK_MAX = 256
ROPE_THETA = 10000.0

LANES = 128
SUBLANES = 8
VMEM_LIMIT = 48 * 1024 * 1024
FLT_MAX = float(np.finfo(np.float32).max)
FLT_TINY = float(np.finfo(np.float32).tiny)
SEARCH_PASS_CAP = 512
NEG_BIG = -1e30
LOG2_E = float(np.log2(np.e))
K_AUG = LANES
M_SLACK = 1.0 + 2.0 ** -5
CNT_ROWS = 4 * SUBLANES
L_FLOOR = 2.0 ** -80
BF16_SUBLANES = 16
PROJ_ROWS = 512
DSA_KEY_BLOCK = 512
BLOCKS_PER_TRIP = 2
PASSES_PER_CHECK = 2
V_ROWS = HEAD_DIM + BF16_SUBLANES


def _cparams(sem):
    return pltpu.CompilerParams(dimension_semantics=sem, vmem_limit_bytes=VMEM_LIMIT)


def _const_spec(shape):
    nd = len(shape)
    return pl.BlockSpec(shape, lambda *_: (0,) * nd)


def _swap_halves(z):
    lane = lax.broadcasted_iota(jnp.int32, z.shape, 1)
    lo_half = (lane % HEAD_DIM) < (HEAD_DIM // 2)
    return jnp.where(lo_half, pltpu.roll(z, LANES - HEAD_DIM // 2, 1), pltpu.roll(z, HEAD_DIM // 2, 1))


def _rope(z, cos, sin):
    outs = []
    for s in range(z.shape[1] // LANES):
        zs = z[:, s * LANES:(s + 1) * LANES]
        outs.append(zs * cos + _swap_halves(zs) * sin)
    return outs[0] if len(outs) == 1 else jnp.concatenate(outs, axis=1)


def _head_rmsnorm(z, gain, g):
    z2 = z * z
    hi = z2.astype(BF16)
    lo = (z2 - hi.astype(F32)).astype(BF16)
    ss = jnp.dot(hi, g, preferred_element_type=F32) + jnp.dot(lo, g, preferred_element_type=F32)
    return z * lax.rsqrt(ss * (1.0 / HEAD_DIM) + EPS) * gain


def _gelu_tanh(x):
    return 0.5 * x * (1.0 + jnp.tanh(np.sqrt(2.0 / np.pi).astype(np.float32) * (x + 0.044715 * (x * x * x))))


def _proj_kernel(x_ref, cos_ref, sin_ref, cprev_ref, hprev_ref,
                 nmix_ref, win_ref, wkw_ref,
                 convw_ref, convb_ref, wrg_ref, wig_ref, brg_ref, big_ref, lam_ref,
                 qn_ref, kn_ref, g_ref,
                 ya_ref, k_ref, v_ref, ki_ref, tail_ref, hlast_ref, *rest, tb, pos0, wi_scale, dsa_layout):
    cbuf, hcar = rest[-2:]
    t = pl.program_id(1)
    w = cbuf.shape[1]

    @pl.when(t == 0)
    def _():
        cbuf[0:SUBLANES, :] = cprev_ref[0]
        hcar[...] = hprev_ref[0]

    x = x_ref[0]
    ms = jnp.mean(x * x, axis=-1, keepdims=True)
    h = (x * lax.rsqrt(ms + EPS) * nmix_ref[...]).astype(BF16)

    o_q = 2 * w
    o_kv = o_q + N_HEADS * HEAD_DIM
    o_qi = o_kv + 2 * N_KV_HEADS * HEAD_DIM
    o_end = o_qi + N_IDX_HEADS * IDX_DIM
    pa = jnp.dot(h, win_ref[:, :o_q], preferred_element_type=F32)
    xr = pa[:, :w]
    gate = pa[:, w:]
    cbuf[SUBLANES:SUBLANES + tb, :] = xr
    xc = convb_ref[...]
    for j in range(CONV_W):
        off = SUBLANES - (CONV_W - 1) + j
        xc = xc + cbuf[off:off + tb, :] * convw_ref[j:j + 1, :]
    tail = cbuf[tb:tb + SUBLANES, :]
    cbuf[0:SUBLANES, :] = tail
    tail_ref[0] = tail

    xcb = xc.astype(BF16)
    r = jax.nn.sigmoid(jnp.dot(xcb, wrg_ref[...], preferred_element_type=F32) + brg_ref[...])
    ig = jax.nn.sigmoid(jnp.dot(xcb, wig_ref[...], preferred_element_type=F32) + big_ref[...])
    nl = -lam_ref[...]
    softplus = jnp.maximum(nl, 0.0) + jnp.log(1.0 + jnp.exp(-jnp.abs(nl)))
    log_a = (-LRU_C) * r * softplus
    a = jnp.exp(log_a)
    mult = jnp.sqrt(1.0 - jnp.exp(2.0 * log_a))
    row = lax.broadcasted_iota(jnp.int32, (tb, 1), 0)
    mult = jnp.where(row + (pos0 + t * tb) == 0, 1.0, mult)
    b = mult * ig * xc
    s = 1
    while s < SUBLANES:
        keep = (row % SUBLANES) >= s
        a_s = jnp.where(keep, pltpu.roll(a, s, 0), 1.0)
        b_s = jnp.where(keep, pltpu.roll(b, s, 0), 0.0)
        b = a * b_s + b
        a = a * a_s
        s *= 2
    h_in = hcar[...]
    groups = []
    for gi in range(tb // SUBLANES):
        rows = slice(gi * SUBLANES, (gi + 1) * SUBLANES)
        hg = a[rows, :] * h_in + b[rows, :]
        groups.append(hg)
        h_in = hg[SUBLANES - 1:SUBLANES, :]
    hs = jnp.concatenate(groups, axis=0)
    hcar[...] = hs[tb - 1:tb, :]
    hlast_ref[0] = hs[tb - SUBLANES:tb, :]
    ya_ref[0] = (hs * _gelu_tanh(gate)).astype(ya_ref.dtype)

    cos = cos_ref[...]
    sin = sin_ref[...]
    g = g_ref[...]
    pq = jnp.dot(h, win_ref[:, o_q:o_kv], preferred_element_type=F32)
    q = _rope(_head_rmsnorm(pq, qn_ref[...], g), cos, sin) * (LOG2_E * HEAD_DIM ** -0.5)

    pkv = jnp.dot(h, win_ref[:, o_kv:o_qi], preferred_element_type=F32)
    kw_ = N_KV_HEADS * HEAD_DIM
    k = _rope(_head_rmsnorm(pkv[:, :kw_], kn_ref[...], g[:kw_, :kw_]), cos, sin)
    v = pkv[:, kw_:]
    k_ref[0] = k
    v_ref[0] = v

    pqi = jnp.dot(h, win_ref[:, o_qi:o_end], preferred_element_type=F32)
    qi = _rope(pqi, cos, sin)

    pkw = jnp.dot(h, wkw_ref[...], preferred_element_type=F32)
    lane = lax.broadcasted_iota(jnp.int32, pkw.shape, 1)
    kwv = jnp.where(lane < IDX_DIM, _rope(pkw, cos, sin), pkw * wi_scale)
    ki_ref[0] = kwv[:, :IDX_DIM]

    if not dsa_layout:
        q_ref, qi_ref, kw_ref = rest[:3]
        q_ref[0] = q.astype(q_ref.dtype)
        qi_ref[0] = qi.astype(qi_ref.dtype)
        kw_ref[0] = kwv
        return

    qg_ref, qiT_ref, wiT_ref, kaug_ref, kib_ref, vT_ref = rest[:6]
    grp = N_HEADS // N_KV_HEADS
    qT = q.T.astype(qg_ref.dtype)
    qg_ref[0, 0] = jnp.concatenate(
        [jnp.concatenate([qT[(gk * grp + hq) * HEAD_DIM:(gk * grp + hq + 1) * HEAD_DIM, :]
                          for hq in range(grp)], axis=1) for gk in range(N_KV_HEADS)], axis=0)
    qiT_ref[0] = qi.T.astype(qiT_ref.dtype)
    wiT_ref[0] = kwv.T[IDX_DIM:IDX_DIM + N_IDX_HEADS, :]
    kib_ref[0] = kwv[:, :IDX_DIM].astype(kib_ref.dtype)
    one_lane = jnp.where(lax.broadcasted_iota(jnp.int32, (tb, K_AUG - HEAD_DIM), 1) == 0, 1.0, 0.0)
    kaug_ref[0] = jnp.concatenate(
        [piece for gk in range(N_KV_HEADS)
         for piece in (k[:, gk * HEAD_DIM:(gk + 1) * HEAD_DIM], one_lane)], axis=1).astype(kaug_ref.dtype)
    vT = v.T
    one_row = jnp.where(lax.broadcasted_iota(jnp.int32, (V_ROWS - HEAD_DIM, tb), 0) == 0, 1.0, 0.0)
    vT_ref[0, 0] = jnp.concatenate(
        [piece for gk in range(N_KV_HEADS)
         for piece in (vT[gk * HEAD_DIM:(gk + 1) * HEAD_DIM, :], one_row)], axis=0).astype(vT_ref.dtype)


def _proj_call(x, cos, sin, cprev, hprev, wts, *, tb, pos0, sk=None):
    bsz, t_len, d = x.shape
    w = wts["conv_b"].shape[1]
    aw = N_HEADS * HEAD_DIM
    kvw = 2 * N_KV_HEADS * HEAD_DIM
    nt = t_len // tb
    wi_scale = float(N_IDX_HEADS ** -0.5 * IDX_DIM ** -0.5)
    names = ("norm_mix", "w_in", "w_kw", "conv_w", "conv_b",
             "w_rg", "w_ig", "b_rg", "b_ig", "lam", "q_norm", "k_norm", "g")
    warrs = [wts[n] for n in names]
    in_specs = [
        pl.BlockSpec((1, tb, d), lambda b, t: (b, t, 0)),
        pl.BlockSpec((tb, LANES), lambda b, t: (t, 0)),
        pl.BlockSpec((tb, LANES), lambda b, t: (t, 0)),
        pl.BlockSpec((1, SUBLANES, w), lambda b, t: (b, 0, 0)),
        pl.BlockSpec((1, 1, w), lambda b, t: (b, 0, 0)),
    ] + [_const_spec(a.shape) for a in warrs]
    out_shape = [
        jax.ShapeDtypeStruct((bsz, t_len, w), BF16),
        jax.ShapeDtypeStruct((bsz, t_len, kvw // 2), F32),
        jax.ShapeDtypeStruct((bsz, t_len, kvw // 2), F32),
        jax.ShapeDtypeStruct((bsz, t_len, IDX_DIM), F32),
        jax.ShapeDtypeStruct((bsz, SUBLANES, w), F32),
        jax.ShapeDtypeStruct((bsz, SUBLANES, w), F32),
    ]
    out_specs = [
        pl.BlockSpec((1, tb, w), lambda b, t: (b, t, 0)),
        pl.BlockSpec((1, tb, kvw // 2), lambda b, t: (b, t, 0)),
        pl.BlockSpec((1, tb, kvw // 2), lambda b, t: (b, t, 0)),
        pl.BlockSpec((1, tb, IDX_DIM), lambda b, t: (b, t, 0)),
        pl.BlockSpec((1, SUBLANES, w), lambda b, t: (b, 0, 0)),
        pl.BlockSpec((1, SUBLANES, w), lambda b, t: (b, 0, 0)),
    ]
    dsa_layout = sk is not None
    if dsa_layout:
        grp = N_HEADS // N_KV_HEADS
        per_sk = sk // tb
        out_shape += [
            jax.ShapeDtypeStruct((bsz, nt, kvw // 2, grp * tb), BF16),
            jax.ShapeDtypeStruct((bsz, aw, t_len), BF16),
            jax.ShapeDtypeStruct((bsz, N_IDX_HEADS, t_len), F32),
            jax.ShapeDtypeStruct((bsz, t_len, N_KV_HEADS * K_AUG), BF16),
            jax.ShapeDtypeStruct((bsz, t_len, IDX_DIM), BF16),
            jax.ShapeDtypeStruct((bsz, t_len // sk, N_KV_HEADS * V_ROWS, sk), BF16),
        ]
        out_specs += [
            pl.BlockSpec((1, 1, kvw // 2, grp * tb), lambda b, t: (b, t, 0, 0)),
            pl.BlockSpec((1, aw, tb), lambda b, t: (b, 0, t)),
            pl.BlockSpec((1, N_IDX_HEADS, tb), lambda b, t: (b, 0, t)),
            pl.BlockSpec((1, tb, N_KV_HEADS * K_AUG), lambda b, t: (b, t, 0)),
            pl.BlockSpec((1, tb, IDX_DIM), lambda b, t: (b, t, 0)),
            pl.BlockSpec((1, 1, N_KV_HEADS * V_ROWS, tb), lambda b, t: (b, t // per_sk, 0, t % per_sk)),
        ]
    else:
        out_shape += [jax.ShapeDtypeStruct((bsz, t_len, aw), BF16),
                      jax.ShapeDtypeStruct((bsz, t_len, aw), BF16),
                      jax.ShapeDtypeStruct((bsz, t_len, LANES), F32)]
        out_specs += [pl.BlockSpec((1, tb, aw), lambda b, t: (b, t, 0)),
                      pl.BlockSpec((1, tb, aw), lambda b, t: (b, t, 0)),
                      pl.BlockSpec((1, tb, LANES), lambda b, t: (b, t, 0))]
    return pl.pallas_call(
        functools.partial(_proj_kernel, tb=tb, pos0=pos0, wi_scale=wi_scale, dsa_layout=dsa_layout),
        grid=(bsz, nt),
        in_specs=in_specs,
        out_specs=out_specs,
        out_shape=out_shape,
        scratch_shapes=[pltpu.VMEM((SUBLANES + tb, w), F32), pltpu.VMEM((1, w), F32)],
        compiler_params=_cparams(("arbitrary", "arbitrary")),
        name="proj",
    )(x, cos, sin, cprev, hprev, *warrs)


def _for_blocks(n, body, init):
    def trip(jj, carry):
        for u in range(BLOCKS_PER_TRIP):
            carry = body(BLOCKS_PER_TRIP * jj + u, carry)
        return carry
    whole = n // BLOCKS_PER_TRIP
    carry = lax.fori_loop(0, whole, trip, init)
    return lax.fori_loop(whole * BLOCKS_PER_TRIP, n, body, carry)


def _dsa_kernel(qT_ref, qiT_ref, wiT_ref, k_ref, vT_ref, ki_ref, kmax_ref, o_ref,
                sc_scr, m_scr, acc_scr, qa_scr, *, tq, sk, s_real, n_kblocks, q_pos0, n_q_real, topk):
    i = pl.program_id(1)
    q_idx = i * tq + lax.broadcasted_iota(jnp.int32, (1, tq), 1)
    q_adm_end = ((q_pos0 + q_idx) // CHUNK + 1) * CHUNK
    q_adm_end = jnp.minimum(q_adm_end, s_real)
    last_q = jnp.minimum(i * tq + tq - 1, n_q_real - 1)
    blk_end = jnp.minimum(((q_pos0 + last_q) // CHUNK + 1) * CHUNK, s_real)
    nkb = jnp.minimum((blk_end + sk - 1) // sk, n_kblocks)

    def score_body(j, carry):
        smax, smin = carry
        off = pl.multiple_of(j * sk, sk)
        kib = ki_ref[0, pl.ds(off, sk), :]
        acc = jnp.zeros((sk, tq), F32)
        for hh in range(N_IDX_HEADS):
            sc = jnp.dot(kib, qiT_ref[0, hh * IDX_DIM:(hh + 1) * IDX_DIM, :], preferred_element_type=F32)
            acc = acc + jnp.maximum(sc, 0.0) * wiT_ref[0, hh:hh + 1, :]
        kidx = off + lax.broadcasted_iota(jnp.int32, (sk, 1), 0)
        adm = kidx < q_adm_end
        hi_part = jnp.where(adm, acc, -jnp.inf)
        lo_part = jnp.where(adm, acc, jnp.inf)
        sc_scr[pl.ds(off, sk), :] = hi_part
        return (jnp.maximum(smax, jnp.max(hi_part.reshape(sk // CNT_ROWS, CNT_ROWS, tq), axis=0)),
                jnp.minimum(smin, jnp.min(lo_part.reshape(sk // CNT_ROWS, CNT_ROWS, tq), axis=0)))

    smax, smin = _for_blocks(
        nkb, score_body,
        (jnp.full((CNT_ROWS, tq), -jnp.inf, F32), jnp.full((CNT_ROWS, tq), jnp.inf, F32)))
    smax = jnp.max(smax, axis=0, keepdims=True)
    smin = jnp.min(smin, axis=0, keepdims=True)

    def count(thr, strict=False):
        def body(j, cnt):
            off = pl.multiple_of(j * sk, sk)
            blk = sc_scr[pl.ds(off, sk), :]
            above = (blk > thr if strict else blk >= thr).astype(jnp.int32)
            return cnt + jnp.sum(above.reshape(sk // CNT_ROWS, CNT_ROWS, tq), axis=0)
        cnt = _for_blocks(nkb, body, jnp.zeros((CNT_ROWS, tq), jnp.int32))
        return jnp.sum(cnt, axis=0, keepdims=True)

    open0 = jnp.logical_and(q_adm_end > topk, q_idx < n_q_real).astype(jnp.int32)

    zero = jnp.zeros((1, tq), F32)
    c_nonneg = count(zero)
    c_pos = count(zero, strict=True)
    is_open0 = open0 > 0
    at_zero = jnp.logical_and(is_open0, jnp.logical_or(
        c_nonneg == topk, jnp.logical_and(c_pos < topk, c_nonneg > topk)))
    tie0 = jnp.logical_and(at_zero, c_nonneg > topk).astype(jnp.int32)
    thr0 = jnp.where(at_zero, 0.0, -FLT_MAX)
    open_a = jnp.where(at_zero, 0, open0)
    above_zero = c_pos >= topk
    lo0 = jnp.where(above_zero, 0.0, smin)
    hi0 = jnp.where(above_zero, smax * (1.0 + 2.0 ** -22) + FLT_TINY, 0.0)

    def search_cond(st):
        it, _, _, _, _, _, n_open = st
        return jnp.logical_and(it < SEARCH_PASS_CAP, n_open > 0)

    def search_step(st):
        lo, hi, thr, opn, tie = st
        mid = 0.5 * lo + 0.5 * hi
        cnt = count(mid)
        is_open = opn > 0
        hit = jnp.logical_and(is_open, cnt == topk)
        stuck = jnp.logical_and(jnp.logical_and(is_open, cnt != topk),
                                jnp.logical_or(mid == lo, mid == hi))
        thr = jnp.where(hit, mid, jnp.where(stuck, lo, thr))
        tie = jnp.where(stuck, 1, tie)
        moving = jnp.logical_and(is_open, jnp.logical_not(jnp.logical_or(hit, stuck)))
        lo = jnp.where(jnp.logical_and(moving, cnt > topk), mid, lo)
        hi = jnp.where(jnp.logical_and(moving, cnt < topk), mid, hi)
        opn = jnp.where(jnp.logical_or(hit, stuck), 0, opn)
        return lo, hi, thr, opn, tie

    def search_body(st):
        it, rest = st[0], st[1:-1]
        for _ in range(PASSES_PER_CHECK):
            rest = search_step(rest)
        return (it + PASSES_PER_CHECK, *rest, jnp.sum(rest[3]))

    _, _, _, thr, _, tie, _ = lax.while_loop(
        search_cond, search_body, (jnp.int32(0), lo0, hi0, thr0, open_a, tie0, jnp.sum(open_a)))

    @pl.when(jnp.sum(tie) > 0)
    def _():
        cnt_gt = count(thr, strict=True)
        need = jnp.where(tie > 0, topk - cnt_gt, jnp.int32(2 ** 30)).astype(F32)
        half = sk // 2
        r_i = lax.broadcasted_iota(jnp.int32, (half, half), 0)
        c_i = lax.broadcasted_iota(jnp.int32, (half, half), 1)
        tri = (c_i <= r_i).astype(BF16)

        def tie_body(j, seen):
            off = pl.multiple_of(j * sk, sk)
            blk = sc_scr[pl.ds(off, sk), :]
            eq = blk == thr
            eqb = eq.astype(BF16)
            top = jnp.dot(tri, eqb[:half], preferred_element_type=F32)
            bot = jnp.dot(tri, eqb[half:], preferred_element_type=F32)
            n_top = top[half - 1:half, :]
            left = need - seen
            over = jnp.concatenate([top > left, bot > left - n_top], axis=0)
            sc_scr[pl.ds(off, sk), :] = jnp.where(jnp.logical_and(eq, over), -jnp.inf, blk)
            return seen + n_top + bot[half - 1:half, :]

        _for_blocks(nkb, tie_body, jnp.zeros((1, tq), F32))

    grp = N_HEADS // N_KV_HEADS

    def finish():
        outs = []
        for gk in range(N_KV_HEADS):
            acc = acc_scr[gk]
            for hq in range(grp):
                sl = slice(hq * tq, (hq + 1) * tq)
                outs.append(acc[:HEAD_DIM, sl] / acc[HEAD_DIM:HEAD_DIM + 1, sl])
        o_ref[0] = jnp.concatenate(outs, axis=0).T.astype(o_ref.dtype)

    for gk in range(N_KV_HEADS):
        qb = qT_ref[0, 0, gk * HEAD_DIM:(gk + 1) * HEAD_DIM, :]
        qf = qb.astype(F32)
        qn2 = jnp.sum(qf * qf, axis=0, keepdims=True)
        m = jnp.sqrt(qn2 * kmax_ref[0, :, gk * K_AUG:gk * K_AUG + 1]) * M_SLACK
        row = lax.broadcasted_iota(jnp.int32, (BF16_SUBLANES, grp * tq), 0)
        qa_scr[gk, 0:HEAD_DIM, :] = qb
        qa_scr[gk, HEAD_DIM:HEAD_DIM + BF16_SUBLANES, :] = jnp.where(row == 0, -m, 0.0).astype(BF16)
        qa_scr[gk, HEAD_DIM + BF16_SUBLANES:, :] = jnp.zeros(
            (K_AUG - HEAD_DIM - BF16_SUBLANES, grp * tq), BF16)
    acc_scr[...] = jnp.zeros(acc_scr.shape, F32)

    def fast_body(j, carry):
        off = pl.multiple_of(j * sk, sk)
        sel = sc_scr[pl.ds(off, sk), :] >= thr
        logits = [jnp.dot(k_ref[0, pl.ds(off, sk), gk * K_AUG:(gk + 1) * K_AUG], qa_scr[gk],
                          preferred_element_type=F32) for gk in range(N_KV_HEADS)]
        for gk in range(N_KV_HEADS):
            s = logits[gk]
            p = jnp.concatenate(
                [jnp.exp2(jnp.where(sel, s[:, hq * tq:(hq + 1) * tq], NEG_BIG)) for hq in range(grp)],
                axis=1).astype(BF16)
            vtb = vT_ref[0, j, gk * V_ROWS:(gk + 1) * V_ROWS, :]
            acc_scr[gk] += jnp.dot(vtb, p, preferred_element_type=F32)
        return carry

    _for_blocks(nkb, fast_body, 0)
    l_min = jnp.min(acc_scr[:, HEAD_DIM:HEAD_DIM + 1, :])
    healthy = l_min > L_FLOOR

    @pl.when(healthy)
    def _():
        finish()

    @pl.when(jnp.logical_not(healthy))
    def _():
        m_scr[...] = jnp.full(m_scr.shape, NEG_BIG, F32)
        acc_scr[...] = jnp.zeros(acc_scr.shape, F32)

        def slow_body(j, carry):
            off = pl.multiple_of(j * sk, sk)
            sel = sc_scr[pl.ds(off, sk), :] >= thr
            for gk in range(N_KV_HEADS):
                kb = k_ref[0, pl.ds(off, sk), gk * K_AUG:gk * K_AUG + HEAD_DIM]
                vtb = vT_ref[0, j, gk * V_ROWS:(gk + 1) * V_ROWS, :]
                s = jnp.dot(kb, qT_ref[0, 0, gk * HEAD_DIM:(gk + 1) * HEAD_DIM, :],
                            preferred_element_type=F32)
                s = jnp.concatenate(
                    [jnp.where(sel, s[:, hq * tq:(hq + 1) * tq], NEG_BIG) for hq in range(grp)], axis=1)
                m_old = m_scr[gk]
                m_new = jnp.maximum(m_old, jnp.max(s, axis=0, keepdims=True))
                p = jnp.exp2(s - m_new).astype(BF16)
                alpha = jnp.exp2(m_old - m_new)
                acc_scr[gk] = alpha * acc_scr[gk] + jnp.dot(vtb, p, preferred_element_type=F32)
                m_scr[gk] = m_new
            return carry

        lax.fori_loop(0, nkb, slow_body, 0)
        finish()


def _kmax_kernel(k_ref, g_ref, o_ref):
    kf = k_ref[0].astype(F32)
    ss = jnp.dot((kf * kf).astype(BF16), g_ref[...], preferred_element_type=F32)
    o_ref[0] = jnp.max(ss, axis=0, keepdims=True)


def _kmax_call(k_b, g):
    bsz, s_pad, kw = k_b.shape
    return pl.pallas_call(
        _kmax_kernel,
        grid=(bsz,),
        in_specs=[pl.BlockSpec((1, s_pad, kw), lambda b: (b, 0, 0)), _const_spec(g.shape)],
        out_specs=pl.BlockSpec((1, 1, kw), lambda b: (b, 0, 0)),
        out_shape=jax.ShapeDtypeStruct((bsz, 1, kw), F32),
        compiler_params=_cparams(("arbitrary",)),
        name="kmax",
    )(k_b, g)


def _keys_kernel(ck_ref, cv_ref, cki_ref, nk_ref, nv_ref, nki_ref, kaug_ref, kib_ref, vT_ref, *, n_cache_blocks):
    j = pl.program_id(1)
    sk = ck_ref.shape[1]
    t_new = nk_ref.shape[1]

    def cached_or_new(c_ref, n_ref):
        new = jnp.concatenate([n_ref[0], jnp.zeros((sk - t_new, n_ref.shape[2]), F32)], axis=0)
        return jnp.where(j < n_cache_blocks, c_ref[0], new)

    k = cached_or_new(ck_ref, nk_ref)
    v = cached_or_new(cv_ref, nv_ref)
    kib_ref[0] = cached_or_new(cki_ref, nki_ref).astype(kib_ref.dtype)
    one_lane = jnp.where(lax.broadcasted_iota(jnp.int32, (sk, K_AUG - HEAD_DIM), 1) == 0, 1.0, 0.0)
    kaug_ref[0] = jnp.concatenate(
        [piece for gk in range(N_KV_HEADS)
         for piece in (k[:, gk * HEAD_DIM:(gk + 1) * HEAD_DIM], one_lane)], axis=1).astype(kaug_ref.dtype)
    vT = v.T
    one_row = jnp.where(lax.broadcasted_iota(jnp.int32, (V_ROWS - HEAD_DIM, sk), 0) == 0, 1.0, 0.0)
    vT_ref[0, 0] = jnp.concatenate(
        [piece for gk in range(N_KV_HEADS)
         for piece in (vT[gk * HEAD_DIM:(gk + 1) * HEAD_DIM, :], one_row)], axis=0).astype(vT_ref.dtype)


def _keys_call(cache_k, cache_v, cache_ki, k_new, v_new, ki_new, *, sk):
    bsz, p_len, kw = cache_k.shape
    t_new = k_new.shape[1]
    ncb = p_len // sk
    nkb = ncb + 1

    def cache_spec(width):
        return pl.BlockSpec((1, sk, width), lambda b, j: (b, jnp.minimum(j, ncb - 1), 0))

    def new_spec(width):
        return pl.BlockSpec((1, t_new, width), lambda b, j: (b, 0, 0))

    return pl.pallas_call(
        functools.partial(_keys_kernel, n_cache_blocks=ncb),
        grid=(bsz, nkb),
        in_specs=[cache_spec(kw), cache_spec(kw), cache_spec(IDX_DIM),
                  new_spec(kw), new_spec(kw), new_spec(IDX_DIM)],
        out_specs=(
            pl.BlockSpec((1, sk, N_KV_HEADS * K_AUG), lambda b, j: (b, j, 0)),
            pl.BlockSpec((1, sk, IDX_DIM), lambda b, j: (b, j, 0)),
            pl.BlockSpec((1, 1, N_KV_HEADS * V_ROWS, sk), lambda b, j: (b, j, 0, 0)),
        ),
        out_shape=(
            jax.ShapeDtypeStruct((bsz, nkb * sk, N_KV_HEADS * K_AUG), BF16),
            jax.ShapeDtypeStruct((bsz, nkb * sk, IDX_DIM), BF16),
            jax.ShapeDtypeStruct((bsz, nkb, N_KV_HEADS * V_ROWS, sk), BF16),
        ),
        compiler_params=_cparams(("arbitrary", "arbitrary")),
        name="keys",
    )(cache_k, cache_v, cache_ki, k_new, v_new, ki_new)


def _dsa_call(qg, qiT, wiT, k_aug, vT_blk, ki_b, kmax, *, tq, sk, s_real, q_pos0, n_q_real, topk):
    bsz, aw, tq_total = qiT.shape
    n_kblocks = vT_blk.shape[1]
    s_pad = k_aug.shape[1]
    nq = tq_total // tq
    grp = N_HEADS // N_KV_HEADS
    kern = functools.partial(_dsa_kernel, tq=tq, sk=sk, s_real=s_real, n_kblocks=n_kblocks,
                             q_pos0=q_pos0, n_q_real=n_q_real, topk=topk)
    return pl.pallas_call(
        kern,
        grid=(bsz, nq),
        in_specs=[
            pl.BlockSpec((1, 1, N_KV_HEADS * HEAD_DIM, grp * tq), lambda b, i: (b, i, 0, 0)),
            pl.BlockSpec((1, aw, tq), lambda b, i: (b, 0, i)),
            pl.BlockSpec((1, N_IDX_HEADS, tq), lambda b, i: (b, 0, i)),
            pl.BlockSpec((1, s_pad, N_KV_HEADS * K_AUG), lambda b, i: (b, 0, 0)),
            pl.BlockSpec((1, n_kblocks, N_KV_HEADS * V_ROWS, sk), lambda b, i: (b, 0, 0, 0)),
            pl.BlockSpec((1, s_pad, IDX_DIM), lambda b, i: (b, 0, 0)),
            pl.BlockSpec((1, 1, N_KV_HEADS * K_AUG), lambda b, i: (b, 0, 0)),
        ],
        out_specs=pl.BlockSpec((1, tq, aw), lambda b, i: (b, i, 0)),
        out_shape=jax.ShapeDtypeStruct((bsz, tq_total, aw), BF16),
        scratch_shapes=[
            pltpu.VMEM((s_pad, tq), F32),
            pltpu.VMEM((N_KV_HEADS, 1, grp * tq), F32),
            pltpu.VMEM((N_KV_HEADS, V_ROWS, grp * tq), F32),
            pltpu.VMEM((N_KV_HEADS, K_AUG, grp * tq), BF16),
        ],
        compiler_params=_cparams(("arbitrary", "arbitrary")),
        name="dsa",
    )(qg, qiT, wiT, k_aug, vT_blk, ki_b, kmax)


def _ffn_kernel(x_ref, ya_ref, yb_ref, wout_ref, nf_ref, wgu_ref, wo_ref, o_ref):
    w = ya_ref.shape[1]
    d_ff = wo_ref.shape[0]
    x1 = x_ref[...] + jnp.dot(ya_ref[...], wout_ref[:w, :], preferred_element_type=F32)
    x1 = x1 + jnp.dot(yb_ref[...], wout_ref[w:, :], preferred_element_type=F32)
    ms = jnp.mean(x1 * x1, axis=-1, keepdims=True)
    hf = (x1 * lax.rsqrt(ms + EPS) * nf_ref[...]).astype(BF16)
    gt = jnp.dot(hf, wgu_ref[:, :d_ff], preferred_element_type=F32)
    up = jnp.dot(hf, wgu_ref[:, d_ff:], preferred_element_type=F32)
    act = (gt * jax.nn.sigmoid(gt) * up).astype(BF16)
    o_ref[...] = x1 + jnp.dot(act, wo_ref[...], preferred_element_type=F32)


def _ffn_call(x2, ya2, yb2, wts, *, tm):
    n, d = x2.shape
    w = ya2.shape[1]
    aw = yb2.shape[1]
    single = pl.Buffered(1)

    def wspec(shape):
        nd = len(shape)
        return pl.BlockSpec(shape, lambda *_: (0,) * nd, pipeline_mode=single)

    names = ("w_out", "norm_ffn", "w_gu", "w_o")
    return pl.pallas_call(
        _ffn_kernel,
        grid=(n // tm,),
        in_specs=[
            pl.BlockSpec((tm, d), lambda r: (r, 0)),
            pl.BlockSpec((tm, w), lambda r: (r, 0)),
            pl.BlockSpec((tm, aw), lambda r: (r, 0)),
        ] + [wspec(wts[k].shape) for k in names],
        out_specs=pl.BlockSpec((tm, d), lambda r: (r, 0)),
        out_shape=jax.ShapeDtypeStruct((n, d), F32),
        compiler_params=_cparams(("arbitrary",)),
        name="ffn",
    )(x2, ya2, yb2, *[wts[k] for k in names])


def _block_diag(wb):
    n, a, b = wb.shape
    eye = jnp.eye(n, dtype=wb.dtype)
    return (eye[:, None, :, None] * wb[:, :, None, :]).reshape(n * a, n * b)


def _prep_weights(norm_mix, w_in, conv_w, conv_b, w_rg, b_rg, w_ig, b_ig, lru_lambda,
                  q_norm, k_norm, w_out, norm_ffn, w_ffn_in, w_ffn_out):
    d = w_in.shape[0]
    w = conv_b.shape[0]
    aw = N_HEADS * HEAD_DIM
    kw = N_KV_HEADS * HEAD_DIM
    iw = N_IDX_HEADS * IDX_DIM
    o = np.cumsum([0, w, w, aw, kw, kw, iw, IDX_DIM, N_IDX_HEADS])
    wb = w_in.astype(BF16)
    w_kw = jnp.pad(wb[:, o[6]:o[8]], ((0, 0), (0, LANES - (IDX_DIM + N_IDX_HEADS))))
    hd_id = np.arange(aw) // HEAD_DIM
    aug_id = np.arange(N_KV_HEADS * K_AUG)
    return {
        "norm_mix": norm_mix.reshape(1, d),
        "w_in": wb,
        "w_kw": w_kw,
        "conv_w": jnp.pad(conv_w, ((0, SUBLANES - CONV_W), (0, 0))),
        "conv_b": conv_b.reshape(1, w),
        "w_rg": _block_diag(w_rg).astype(BF16),
        "w_ig": _block_diag(w_ig).astype(BF16),
        "b_rg": b_rg.reshape(1, w),
        "b_ig": b_ig.reshape(1, w),
        "lam": lru_lambda.reshape(1, w),
        "q_norm": jnp.tile(q_norm, N_HEADS).reshape(1, aw),
        "k_norm": jnp.tile(k_norm, N_KV_HEADS).reshape(1, kw),
        "g": jnp.asarray(hd_id[:, None] == hd_id[None, :], BF16),
        "g_aug": jnp.asarray((aug_id[:, None] // K_AUG == aug_id[None, :] // K_AUG)
                             & (aug_id[:, None] % K_AUG < HEAD_DIM), BF16),
        "w_out": w_out.astype(BF16),
        "norm_ffn": norm_ffn.reshape(1, d),
        "w_gu": w_ffn_in.astype(BF16),
        "w_o": w_ffn_out.astype(BF16),
    }


def _rope_tables(pos):
    half = HEAD_DIM // 2
    inv = ROPE_THETA ** (-jnp.arange(half, dtype=F32) / half)
    ang = pos.astype(F32)[:, None] * inv[None, :]
    cos = jnp.cos(ang)
    sin = jnp.sin(ang)
    reps = LANES // HEAD_DIM
    return (jnp.tile(jnp.concatenate([cos, cos], axis=1), (1, reps)),
            jnp.tile(jnp.concatenate([-sin, sin], axis=1), (1, reps)))


def _query_operands(q, qi, wi, tq):
    bsz, t_len, _ = q.shape
    pad_q = ((0, 0), (0, tq - t_len), (0, 0))
    grp = N_HEADS // N_KV_HEADS
    qg = jnp.pad(q, pad_q).reshape(bsz, 1, tq, N_KV_HEADS, grp, HEAD_DIM)
    qg = qg.transpose(0, 1, 3, 5, 4, 2).reshape(bsz, 1, N_KV_HEADS * HEAD_DIM, grp * tq)
    return qg, jnp.pad(qi, pad_q).transpose(0, 2, 1), jnp.pad(wi, pad_q).transpose(0, 2, 1)


def _layer(x, pos0, conv_prev, h_prev, k_past, v_past, ki_past, wts):
    bsz, t_len, d = x.shape
    w = conv_prev.shape[2]
    kw = N_KV_HEADS * HEAD_DIM
    pos = pos0 + jnp.arange(t_len, dtype=jnp.int32)
    cos, sin = _rope_tables(pos)
    sk = DSA_KEY_BLOCK
    cprev = jnp.pad(conv_prev, ((0, 0), (SUBLANES - (CONV_W - 1), 0), (0, 0)))
    hprev = h_prev.reshape(bsz, 1, w)
    if k_past is None:
        assert t_len % sk == 0 and t_len % PROJ_ROWS == 0, (t_len, sk, PROJ_ROWS)
        tq = PROJ_ROWS
        outs = _proj_call(x, cos, sin, cprev, hprev, wts, tb=PROJ_ROWS, pos0=pos0, sk=sk)
        ya, k_new, v_new, ki_new, tail, hlast, qg, qiT, wiT, k_aug, ki_b, vT_blk = outs
        s_real = t_len
    else:
        p_len = k_past.shape[1]
        tq = LANES
        assert p_len % sk == 0 and t_len <= min(sk, tq), (p_len, t_len, sk)
        outs = _proj_call(x, cos, sin, cprev, hprev, wts, tb=t_len, pos0=pos0)
        ya, k_new, v_new, ki_new, tail, hlast, q, qi, kwi = outs
        qg, qiT, wiT = _query_operands(q, qi, kwi[..., IDX_DIM:IDX_DIM + N_IDX_HEADS], tq)
        k_aug, ki_b, vT_blk = _keys_call(k_past.reshape(bsz, p_len, kw), v_past.reshape(bsz, p_len, kw),
                                         ki_past, k_new, v_new, ki_new, sk=sk)
        s_real = p_len + t_len
    topk = min(TOPK_MAX, s_real // 4)
    kmax = _kmax_call(k_aug, wts["g_aug"])
    yb = _dsa_call(qg, qiT, wiT, k_aug, vT_blk, ki_b, kmax, tq=tq, sk=sk, s_real=s_real,
                   q_pos0=pos0, n_q_real=t_len, topk=topk)[:, :t_len]

    n = bsz * t_len
    tm = 512 if n % 512 == 0 else n
    y = _ffn_call(x.reshape(n, d), ya.reshape(n, w), yb.reshape(n, N_HEADS * HEAD_DIM), wts, tm=tm)
    return (y.reshape(bsz, t_len, d),
            k_new.reshape(bsz, t_len, N_KV_HEADS, HEAD_DIM),
            v_new.reshape(bsz, t_len, N_KV_HEADS, HEAD_DIM),
            ki_new,
            hlast[:, SUBLANES - 1],
            tail[:, SUBLANES - (CONV_W - 1):])


def kernel(x_prompt, x_sample, cache_k, cache_v, cache_kidx, state_h, state_conv, norm_mix, w_in, conv_w, conv_b, w_rg, b_rg, w_ig, b_ig, lru_lambda, q_norm, k_norm, w_out, norm_ffn, w_ffn_in, w_ffn_out):
    wts = _prep_weights(norm_mix, w_in, conv_w, conv_b, w_rg, b_rg, w_ig, b_ig, lru_lambda,
                        q_norm, k_norm, w_out, norm_ffn, w_ffn_in, w_ffn_out)
    bp = x_prompt.shape[0]
    w = conv_b.shape[0]
    p_len = cache_k.shape[1]
    conv0 = jnp.zeros((bp, CONV_W - 1, w), x_prompt.dtype)
    h0 = jnp.zeros((bp, w), x_prompt.dtype)
    yp, k_p, v_p, ki_p, h_p, conv_p = _layer(x_prompt, 0, conv0, h0, None, None, None, wts)
    ys, k_s, v_s, ki_s, h_s, conv_s = _layer(x_sample, p_len, state_conv, state_h,
                                             cache_k, cache_v, cache_kidx, wts)
    return (yp, ys, k_p, v_p, ki_p, h_p, conv_p, k_s, v_s, ki_s, h_s, conv_s)
```

```python
import functools

import numpy as np
import jax
import jax.numpy as jnp
from jax import lax
from jax.experimental import pallas as pl
from jax.experimental.pallas import tpu as pltpu

F32 = jnp.float32
BF16 = jnp.bfloat16

CHUNK = 64
EPS = 1e-6
CONV_W = 4
LRU_C = 8.0
N_HEADS = 8
N_KV_HEADS = 2
HEAD_DIM = 64
N_IDX_HEADS = 8
IDX_DIM = 64
TOPK_MAX = 256
ROPE_THETA = 10000.0

LANES = 128
SUBLANES = 8
VMEM_LIMIT = 48 * 1024 * 1024
FLT_MAX = float(np.finfo(np.float32).max)
FLT_TINY = float(np.finfo(np.float32).tiny)
SEARCH_PASS_CAP = 512
NEG_BIG = -1e30
LOG2_E = float(np.log2(np.e))
K_AUG = LANES
M_SLACK = 1.0 + 2.0 ** -5
CNT_ROWS = 4 * SUBLANES
L_FLOOR = 2.0 ** -80
BF16_SUBLANES = 16
PROJ_ROWS = 512
DSA_KEY_BLOCK = 512
BLOCKS_PER_TRIP = 2
PASSES_PER_CHECK = 2
V_ROWS = HEAD_DIM + BF16_SUBLANES


def _cparams(sem):
    return pltpu.CompilerParams(dimension_semantics=sem, vmem_limit_bytes=VMEM_LIMIT)


def _const_spec(shape):
    nd = len(shape)
    return pl.BlockSpec(shape, lambda *_: (0,) * nd)


def _swap_halves(z):
    lane = lax.broadcasted_iota(jnp.int32, z.shape, 1)
    lo_half = (lane % HEAD_DIM) < (HEAD_DIM // 2)
    return jnp.where(lo_half, pltpu.roll(z, LANES - HEAD_DIM // 2, 1), pltpu.roll(z, HEAD_DIM // 2, 1))


def _rope(z, cos, sin):
    outs = []
    for s in range(z.shape[1] // LANES):
        zs = z[:, s * LANES:(s + 1) * LANES]
        outs.append(zs * cos + _swap_halves(zs) * sin)
    return outs[0] if len(outs) == 1 else jnp.concatenate(outs, axis=1)


def _head_rmsnorm(z, gain, g):
    z2 = z * z
    hi = z2.astype(BF16)
    lo = (z2 - hi.astype(F32)).astype(BF16)
    ss = jnp.dot(hi, g, preferred_element_type=F32) + jnp.dot(lo, g, preferred_element_type=F32)
    return z * lax.rsqrt(ss * (1.0 / HEAD_DIM) + EPS) * gain


def _gelu_tanh(x):
    return 0.5 * x * (1.0 + jnp.tanh(np.sqrt(2.0 / np.pi).astype(np.float32) * (x + 0.044715 * (x * x * x))))


def _proj_kernel(x_ref, cos_ref, sin_ref, cprev_ref, hprev_ref,
                 nmix_ref, win_ref, wkw_ref,
                 convw_ref, convb_ref, wrg_ref, wig_ref, brg_ref, big_ref, lam_ref,
                 qn_ref, kn_ref, g_ref,
                 ya_ref, k_ref, v_ref, ki_ref, tail_ref, hlast_ref, *rest, tb, pos0, wi_scale, dsa_layout):
    cbuf, hcar = rest[-2:]
    t = pl.program_id(1)
    w = cbuf.shape[1]

    @pl.when(t == 0)
    def _():
        cbuf[0:SUBLANES, :] = cprev_ref[0]
        hcar[...] = hprev_ref[0]

    x = x_ref[0]
    ms = jnp.mean(x * x, axis=-1, keepdims=True)
    h = (x * lax.rsqrt(ms + EPS) * nmix_ref[...]).astype(BF16)

    o_q = 2 * w
    o_kv = o_q + N_HEADS * HEAD_DIM
    o_qi = o_kv + 2 * N_KV_HEADS * HEAD_DIM
    o_end = o_qi + N_IDX_HEADS * IDX_DIM
    pa = jnp.dot(h, win_ref[:, :o_q], preferred_element_type=F32)
    pq = jnp.dot(h, win_ref[:, o_q:o_kv], preferred_element_type=F32)
    pkv = jnp.dot(h, win_ref[:, o_kv:o_qi], preferred_element_type=F32)
    pqi = jnp.dot(h, win_ref[:, o_qi:o_end], preferred_element_type=F32)
    pkw = jnp.dot(h, wkw_ref[...], preferred_element_type=F32)
    xr = pa[:, :w]
    gate = pa[:, w:]
    cbuf[SUBLANES:SUBLANES + tb, :] = xr
    xc = convb_ref[...]
    for j in range(CONV_W):
        off = SUBLANES - (CONV_W - 1) + j
        xc = xc + cbuf[off:off + tb, :] * convw_ref[j:j + 1, :]
    tail = cbuf[tb:tb + SUBLANES, :]
    cbuf[0:SUBLANES, :] = tail
    tail_ref[0] = tail

    xcb = xc.astype(BF16)
    r = jax.nn.sigmoid(jnp.dot(xcb, wrg_ref[...], preferred_element_type=F32) + brg_ref[...])
    ig = jax.nn.sigmoid(jnp.dot(xcb, wig_ref[...], preferred_element_type=F32) + big_ref[...])
    nl = -lam_ref[...]
    softplus = jnp.maximum(nl, 0.0) + jnp.log(1.0 + jnp.exp(-jnp.abs(nl)))
    log_a = (-LRU_C) * r * softplus
    a = jnp.exp(log_a)
    mult = jnp.sqrt(1.0 - jnp.exp(2.0 * log_a))
    row = lax.broadcasted_iota(jnp.int32, (tb, 1), 0)
    mult = jnp.where(row + (pos0 + t * tb) == 0, 1.0, mult)
    b = mult * ig * xc
    s = 1
    while s < SUBLANES:
        keep = (row % SUBLANES) >= s
        a_s = jnp.where(keep, pltpu.roll(a, s, 0), 1.0)
        b_s = jnp.where(keep, pltpu.roll(b, s, 0), 0.0)
        b = a * b_s + b
        a = a * a_s
        s *= 2
    h_in = hcar[...]
    groups = []
    for gi in range(tb // SUBLANES):
        rows = slice(gi * SUBLANES, (gi + 1) * SUBLANES)
        hg = a[rows, :] * h_in + b[rows, :]
        groups.append(hg)
        h_in = hg[SUBLANES - 1:SUBLANES, :]
    hs = jnp.concatenate(groups, axis=0)
    hcar[...] = hs[tb - 1:tb, :]
    hlast_ref[0] = hs[tb - SUBLANES:tb, :]
    ya_ref[0] = (hs * _gelu_tanh(gate)).astype(ya_ref.dtype)

    cos = cos_ref[...]
    sin = sin_ref[...]
    g = g_ref[...]
    q = _rope(_head_rmsnorm(pq, qn_ref[...], g), cos, sin) * (LOG2_E * HEAD_DIM ** -0.5)

    kw_ = N_KV_HEADS * HEAD_DIM
    k = _rope(_head_rmsnorm(pkv[:, :kw_], kn_ref[...], g[:kw_, :kw_]), cos, sin)
    v = pkv[:, kw_:]
    k_ref[0] = k
    v_ref[0] = v

    qi = _rope(pqi, cos, sin)

    lane = lax.broadcasted_iota(jnp.int32, pkw.shape, 1)
    kwv = jnp.where(lane < IDX_DIM, _rope(pkw, cos, sin), pkw * wi_scale)
    ki_ref[0] = kwv[:, :IDX_DIM]

    if not dsa_layout:
        q_ref, qi_ref, kw_ref = rest[:3]
        q_ref[0] = q.astype(q_ref.dtype)
        qi_ref[0] = qi.astype(qi_ref.dtype)
        kw_ref[0] = kwv
        return

    qg_ref, qiT_ref, wiT_ref, kaug_ref, kib_ref, vT_ref = rest[:6]
    grp = N_HEADS // N_KV_HEADS
    qT = q.T.astype(qg_ref.dtype)
    qg_ref[0, 0] = jnp.concatenate(
        [jnp.concatenate([qT[(gk * grp + hq) * HEAD_DIM:(gk * grp + hq + 1) * HEAD_DIM, :]
                          for hq in range(grp)], axis=1) for gk in range(N_KV_HEADS)], axis=0)
    qiT_ref[0] = qi.T.astype(qiT_ref.dtype)
    wiT_ref[0] = kwv.T[IDX_DIM:IDX_DIM + N_IDX_HEADS, :]
    kib_ref[0] = kwv[:, :IDX_DIM].astype(kib_ref.dtype)
    one_lane = jnp.where(lax.broadcasted_iota(jnp.int32, (tb, K_AUG - HEAD_DIM), 1) == 0, 1.0, 0.0)
    kaug_ref[0] = jnp.concatenate(
        [piece for gk in range(N_KV_HEADS)
         for piece in (k[:, gk * HEAD_DIM:(gk + 1) * HEAD_DIM], one_lane)], axis=1).astype(kaug_ref.dtype)
    vT = v.T
    one_row = jnp.where(lax.broadcasted_iota(jnp.int32, (V_ROWS - HEAD_DIM, tb), 0) == 0, 1.0, 0.0)
    vT_ref[0, 0] = jnp.concatenate(
        [piece for gk in range(N_KV_HEADS)
         for piece in (vT[gk * HEAD_DIM:(gk + 1) * HEAD_DIM, :], one_row)], axis=0).astype(vT_ref.dtype)


def _proj_call(x, cos, sin, cprev, hprev, wts, *, tb, pos0, sk=None):
    bsz, t_len, d = x.shape
    w = wts["conv_b"].shape[1]
    aw = N_HEADS * HEAD_DIM
    kvw = 2 * N_KV_HEADS * HEAD_DIM
    nt = t_len // tb
    wi_scale = float(N_IDX_HEADS ** -0.5 * IDX_DIM ** -0.5)
    names = ("norm_mix", "w_in", "w_kw", "conv_w", "conv_b",
             "w_rg", "w_ig", "b_rg", "b_ig", "lam", "q_norm", "k_norm", "g")
    warrs = [wts[n] for n in names]
    in_specs = [
        pl.BlockSpec((1, tb, d), lambda b, t: (b, t, 0)),
        pl.BlockSpec((tb, LANES), lambda b, t: (t, 0)),
        pl.BlockSpec((tb, LANES), lambda b, t: (t, 0)),
        pl.BlockSpec((1, SUBLANES, w), lambda b, t: (b, 0, 0)),
        pl.BlockSpec((1, 1, w), lambda b, t: (b, 0, 0)),
    ] + [_const_spec(a.shape) for a in warrs]
    out_shape = [
        jax.ShapeDtypeStruct((bsz, t_len, w), BF16),
        jax.ShapeDtypeStruct((bsz, t_len, kvw // 2), F32),
        jax.ShapeDtypeStruct((bsz, t_len, kvw // 2), F32),
        jax.ShapeDtypeStruct((bsz, t_len, IDX_DIM), F32),
        jax.ShapeDtypeStruct((bsz, SUBLANES, w), F32),
        jax.ShapeDtypeStruct((bsz, SUBLANES, w), F32),
    ]
    out_specs = [
        pl.BlockSpec((1, tb, w), lambda b, t: (b, t, 0)),
        pl.BlockSpec((1, tb, kvw // 2), lambda b, t: (b, t, 0)),
        pl.BlockSpec((1, tb, kvw // 2), lambda b, t: (b, t, 0)),
        pl.BlockSpec((1, tb, IDX_DIM), lambda b, t: (b, t, 0)),
        pl.BlockSpec((1, SUBLANES, w), lambda b, t: (b, 0, 0)),
        pl.BlockSpec((1, SUBLANES, w), lambda b, t: (b, 0, 0)),
    ]
    dsa_layout = sk is not None
    if dsa_layout:
        grp = N_HEADS // N_KV_HEADS
        per_sk = sk // tb
        out_shape += [
            jax.ShapeDtypeStruct((bsz, nt, kvw // 2, grp * tb), BF16),
            jax.ShapeDtypeStruct((bsz, aw, t_len), BF16),
            jax.ShapeDtypeStruct((bsz, N_IDX_HEADS, t_len), F32),
            jax.ShapeDtypeStruct((bsz, t_len, N_KV_HEADS * K_AUG), BF16),
            jax.ShapeDtypeStruct((bsz, t_len, IDX_DIM), BF16),
            jax.ShapeDtypeStruct((bsz, t_len // sk, N_KV_HEADS * V_ROWS, sk), BF16),
        ]
        out_specs += [
            pl.BlockSpec((1, 1, kvw // 2, grp * tb), lambda b, t: (b, t, 0, 0)),
            pl.BlockSpec((1, aw, tb), lambda b, t: (b, 0, t)),
            pl.BlockSpec((1, N_IDX_HEADS, tb), lambda b, t: (b, 0, t)),
            pl.BlockSpec((1, tb, N_KV_HEADS * K_AUG), lambda b, t: (b, t, 0)),
            pl.BlockSpec((1, tb, IDX_DIM), lambda b, t: (b, t, 0)),
            pl.BlockSpec((1, 1, N_KV_HEADS * V_ROWS, tb), lambda b, t: (b, t // per_sk, 0, t % per_sk)),
        ]
    else:
        out_shape += [jax.ShapeDtypeStruct((bsz, t_len, aw), BF16),
                      jax.ShapeDtypeStruct((bsz, t_len, aw), BF16),
                      jax.ShapeDtypeStruct((bsz, t_len, LANES), F32)]
        out_specs += [pl.BlockSpec((1, tb, aw), lambda b, t: (b, t, 0)),
                      pl.BlockSpec((1, tb, aw), lambda b, t: (b, t, 0)),
                      pl.BlockSpec((1, tb, LANES), lambda b, t: (b, t, 0))]
    return pl.pallas_call(
        functools.partial(_proj_kernel, tb=tb, pos0=pos0, wi_scale=wi_scale, dsa_layout=dsa_layout),
        grid=(bsz, nt),
        in_specs=in_specs,
        out_specs=out_specs,
        out_shape=out_shape,
        scratch_shapes=[pltpu.VMEM((SUBLANES + tb, w), F32), pltpu.VMEM((1, w), F32)],
        compiler_params=_cparams(("arbitrary", "arbitrary")),
        name="proj",
    )(x, cos, sin, cprev, hprev, *warrs)


def _for_blocks(n, body, init):
    def trip(jj, carry):
        for u in range(BLOCKS_PER_TRIP):
            carry = body(BLOCKS_PER_TRIP * jj + u, carry)
        return carry
    whole = n // BLOCKS_PER_TRIP
    carry = lax.fori_loop(0, whole, trip, init)
    return lax.fori_loop(whole * BLOCKS_PER_TRIP, n, body, carry)


def _dsa_kernel(qT_ref, qiT_ref, wiT_ref, k_ref, vT_ref, ki_ref, kmax_ref, o_ref,
                sc_scr, m_scr, acc_scr, qa_scr, *, tq, sk, s_real, n_kblocks, q_pos0, n_q_real, topk):
    i = pl.program_id(1)
    q_idx = i * tq + lax.broadcasted_iota(jnp.int32, (1, tq), 1)
    q_adm_end = ((q_pos0 + q_idx) // CHUNK + 1) * CHUNK
    q_adm_end = jnp.minimum(q_adm_end, s_real)
    last_q = jnp.minimum(i * tq + tq - 1, n_q_real - 1)
    blk_end = jnp.minimum(((q_pos0 + last_q) // CHUNK + 1) * CHUNK, s_real)
    nkb = jnp.minimum((blk_end + sk - 1) // sk, n_kblocks)

    def score_body(j, carry):
        smax, smin = carry
        off = pl.multiple_of(j * sk, sk)
        kib = ki_ref[0, pl.ds(off, sk), :]
        acc = jnp.zeros((sk, tq), F32)
        for hh in range(N_IDX_HEADS):
            sc = jnp.dot(kib, qiT_ref[0, hh * IDX_DIM:(hh + 1) * IDX_DIM, :], preferred_element_type=F32)
            acc = acc + jnp.maximum(sc, 0.0) * wiT_ref[0, hh:hh + 1, :]
        kidx = off + lax.broadcasted_iota(jnp.int32, (sk, 1), 0)
        adm = kidx < q_adm_end
        hi_part = jnp.where(adm, acc, -jnp.inf)
        lo_part = jnp.where(adm, acc, jnp.inf)
        sc_scr[pl.ds(off, sk), :] = hi_part
        return (jnp.maximum(smax, jnp.max(hi_part.reshape(sk // CNT_ROWS, CNT_ROWS, tq), axis=0)),
                jnp.minimum(smin, jnp.min(lo_part.reshape(sk // CNT_ROWS, CNT_ROWS, tq), axis=0)))

    smax, smin = _for_blocks(
        nkb, score_body,
        (jnp.full((CNT_ROWS, tq), -jnp.inf, F32), jnp.full((CNT_ROWS, tq), jnp.inf, F32)))
    smax = jnp.max(smax, axis=0, keepdims=True)
    smin = jnp.min(smin, axis=0, keepdims=True)

    def count(thr, strict=False):
        def body(j, cnt):
            off = pl.multiple_of(j * sk, sk)
            blk = sc_scr[pl.ds(off, sk), :]
            above = (blk > thr if strict else blk >= thr).astype(jnp.int32)
            return cnt + jnp.sum(above.reshape(sk // CNT_ROWS, CNT_ROWS, tq), axis=0)
        cnt = _for_blocks(nkb, body, jnp.zeros((CNT_ROWS, tq), jnp.int32))
        return jnp.sum(cnt, axis=0, keepdims=True)

    open0 = jnp.logical_and(q_adm_end > topk, q_idx < n_q_real).astype(jnp.int32)

    zero = jnp.zeros((1, tq), F32)
    c_nonneg = count(zero)
    c_pos = count(zero, strict=True)
    is_open0 = open0 > 0
    at_zero = jnp.logical_and(is_open0, jnp.logical_or(
        c_nonneg == topk, jnp.logical_and(c_pos < topk, c_nonneg > topk)))
    tie0 = jnp.logical_and(at_zero, c_nonneg > topk).astype(jnp.int32)
    thr0 = jnp.where(at_zero, 0.0, -FLT_MAX)
    open_a = jnp.where(at_zero, 0, open0)
    above_zero = c_pos >= topk
    lo0 = jnp.where(above_zero, 0.0, smin)
    hi0 = jnp.where(above_zero, smax * (1.0 + 2.0 ** -22) + FLT_TINY, 0.0)

    def search_cond(st):
        it, _, _, _, _, _, n_open = st
        return jnp.logical_and(it < SEARCH_PASS_CAP, n_open > 0)

    def search_step(st):
        lo, hi, thr, opn, tie = st
        mid = 0.5 * lo + 0.5 * hi
        cnt = count(mid)
        is_open = opn > 0
        hit = jnp.logical_and(is_open, cnt == topk)
        stuck = jnp.logical_and(jnp.logical_and(is_open, cnt != topk),
                                jnp.logical_or(mid == lo, mid == hi))
        thr = jnp.where(hit, mid, jnp.where(stuck, lo, thr))
        tie = jnp.where(stuck, 1, tie)
        moving = jnp.logical_and(is_open, jnp.logical_not(jnp.logical_or(hit, stuck)))
        lo = jnp.where(jnp.logical_and(moving, cnt > topk), mid, lo)
        hi = jnp.where(jnp.logical_and(moving, cnt < topk), mid, hi)
        opn = jnp.where(jnp.logical_or(hit, stuck), 0, opn)
        return lo, hi, thr, opn, tie

    def search_body(st):
        it, rest = st[0], st[1:-1]
        for _ in range(PASSES_PER_CHECK):
            rest = search_step(rest)
        return (it + PASSES_PER_CHECK, *rest, jnp.sum(rest[3]))

    _, _, _, thr, _, tie, _ = lax.while_loop(
        search_cond, search_body, (jnp.int32(0), lo0, hi0, thr0, open_a, tie0, jnp.sum(open_a)))

    @pl.when(jnp.sum(tie) > 0)
    def _():
        cnt_gt = count(thr, strict=True)
        need = jnp.where(tie > 0, topk - cnt_gt, jnp.int32(2 ** 30)).astype(F32)
        half = sk // 2
        r_i = lax.broadcasted_iota(jnp.int32, (half, half), 0)
        c_i = lax.broadcasted_iota(jnp.int32, (half, half), 1)
        tri = (c_i <= r_i).astype(BF16)

        def tie_body(j, seen):
            off = pl.multiple_of(j * sk, sk)
            blk = sc_scr[pl.ds(off, sk), :]
            eq = blk == thr
            eqb = eq.astype(BF16)
            top = jnp.dot(tri, eqb[:half], preferred_element_type=F32)
            bot = jnp.dot(tri, eqb[half:], preferred_element_type=F32)
            n_top = top[half - 1:half, :]
            left = need - seen
            over = jnp.concatenate([top > left, bot > left - n_top], axis=0)
            sc_scr[pl.ds(off, sk), :] = jnp.where(jnp.logical_and(eq, over), -jnp.inf, blk)
            return seen + n_top + bot[half - 1:half, :]

        _for_blocks(nkb, tie_body, jnp.zeros((1, tq), F32))

    grp = N_HEADS // N_KV_HEADS

    def finish():
        outs = []
        for gk in range(N_KV_HEADS):
            acc = acc_scr[gk]
            for hq in range(grp):
                sl = slice(hq * tq, (hq + 1) * tq)
                outs.append(acc[:HEAD_DIM, sl] / acc[HEAD_DIM:HEAD_DIM + 1, sl])
        o_ref[0] = jnp.concatenate(outs, axis=0).T.astype(o_ref.dtype)

    for gk in range(N_KV_HEADS):
        qb = qT_ref[0, 0, gk * HEAD_DIM:(gk + 1) * HEAD_DIM, :]
        qf = qb.astype(F32)
        qn2 = jnp.sum(qf * qf, axis=0, keepdims=True)
        m = jnp.sqrt(qn2 * kmax_ref[0, :, gk * K_AUG:gk * K_AUG + 1]) * M_SLACK
        row = lax.broadcasted_iota(jnp.int32, (BF16_SUBLANES, grp * tq), 0)
        qa_scr[gk, 0:HEAD_DIM, :] = qb
        qa_scr[gk, HEAD_DIM:HEAD_DIM + BF16_SUBLANES, :] = jnp.where(row == 0, -m, 0.0).astype(BF16)
        qa_scr[gk, HEAD_DIM + BF16_SUBLANES:, :] = jnp.zeros(
            (K_AUG - HEAD_DIM - BF16_SUBLANES, grp * tq), BF16)
    acc_scr[...] = jnp.zeros(acc_scr.shape, F32)

    def fast_body(j, carry):
        off = pl.multiple_of(j * sk, sk)
        sel = sc_scr[pl.ds(off, sk), :] >= thr
        logits = [jnp.dot(k_ref[0, pl.ds(off, sk), gk * K_AUG:(gk + 1) * K_AUG], qa_scr[gk],
                          preferred_element_type=F32) for gk in range(N_KV_HEADS)]
        for gk in range(N_KV_HEADS):
            s = logits[gk]
            p = jnp.concatenate(
                [jnp.exp2(jnp.where(sel, s[:, hq * tq:(hq + 1) * tq], NEG_BIG)) for hq in range(grp)],
                axis=1).astype(BF16)
            vtb = vT_ref[0, j, gk * V_ROWS:(gk + 1) * V_ROWS, :]
            acc_scr[gk] += jnp.dot(vtb, p, preferred_element_type=F32)
        return carry

    _for_blocks(nkb, fast_body, 0)
    l_min = jnp.min(acc_scr[:, HEAD_DIM:HEAD_DIM + 1, :])
    healthy = l_min > L_FLOOR

    @pl.when(healthy)
    def _():
        finish()

    @pl.when(jnp.logical_not(healthy))
    def _():
        m_scr[...] = jnp.full(m_scr.shape, NEG_BIG, F32)
        acc_scr[...] = jnp.zeros(acc_scr.shape, F32)

        def slow_body(j, carry):
            off = pl.multiple_of(j * sk, sk)
            sel = sc_scr[pl.ds(off, sk), :] >= thr
            for gk in range(N_KV_HEADS):
                kb = k_ref[0, pl.ds(off, sk), gk * K_AUG:gk * K_AUG + HEAD_DIM]
                vtb = vT_ref[0, j, gk * V_ROWS:(gk + 1) * V_ROWS, :]
                s = jnp.dot(kb, qT_ref[0, 0, gk * HEAD_DIM:(gk + 1) * HEAD_DIM, :],
                            preferred_element_type=F32)
                s = jnp.concatenate(
                    [jnp.where(sel, s[:, hq * tq:(hq + 1) * tq], NEG_BIG) for hq in range(grp)], axis=1)
                m_old = m_scr[gk]
                m_new = jnp.maximum(m_old, jnp.max(s, axis=0, keepdims=True))
                p = jnp.exp2(s - m_new).astype(BF16)
                alpha = jnp.exp2(m_old - m_new)
                acc_scr[gk] = alpha * acc_scr[gk] + jnp.dot(vtb, p, preferred_element_type=F32)
                m_scr[gk] = m_new
            return carry

        lax.fori_loop(0, nkb, slow_body, 0)
        finish()


def _kmax_kernel(k_ref, g_ref, o_ref):
    kf = k_ref[0].astype(F32)
    ss = jnp.dot((kf * kf).astype(BF16), g_ref[...], preferred_element_type=F32)
    o_ref[0] = jnp.max(ss, axis=0, keepdims=True)


def _kmax_call(k_b, g):
    bsz, s_pad, kw = k_b.shape
    return pl.pallas_call(
        _kmax_kernel,
        grid=(bsz,),
        in_specs=[pl.BlockSpec((1, s_pad, kw), lambda b: (b, 0, 0)), _const_spec(g.shape)],
        out_specs=pl.BlockSpec((1, 1, kw), lambda b: (b, 0, 0)),
        out_shape=jax.ShapeDtypeStruct((bsz, 1, kw), F32),
        compiler_params=_cparams(("arbitrary",)),
        name="kmax",
    )(k_b, g)


def _keys_kernel(ck_ref, cv_ref, cki_ref, nk_ref, nv_ref, nki_ref, kaug_ref, kib_ref, vT_ref, *, n_cache_blocks):
    j = pl.program_id(1)
    sk = ck_ref.shape[1]
    t_new = nk_ref.shape[1]

    def cached_or_new(c_ref, n_ref):
        new = jnp.concatenate([n_ref[0], jnp.zeros((sk - t_new, n_ref.shape[2]), F32)], axis=0)
        return jnp.where(j < n_cache_blocks, c_ref[0], new)

    k = cached_or_new(ck_ref, nk_ref)
    v = cached_or_new(cv_ref, nv_ref)
    kib_ref[0] = cached_or_new(cki_ref, nki_ref).astype(kib_ref.dtype)
    one_lane = jnp.where(lax.broadcasted_iota(jnp.int32, (sk, K_AUG - HEAD_DIM), 1) == 0, 1.0, 0.0)
    kaug_ref[0] = jnp.concatenate(
        [piece for gk in range(N_KV_HEADS)
         for piece in (k[:, gk * HEAD_DIM:(gk + 1) * HEAD_DIM], one_lane)], axis=1).astype(kaug_ref.dtype)
    vT = v.T
    one_row = jnp.where(lax.broadcasted_iota(jnp.int32, (V_ROWS - HEAD_DIM, sk), 0) == 0, 1.0, 0.0)
    vT_ref[0, 0] = jnp.concatenate(
        [piece for gk in range(N_KV_HEADS)
         for piece in (vT[gk * HEAD_DIM:(gk + 1) * HEAD_DIM, :], one_row)], axis=0).astype(vT_ref.dtype)


def _keys_call(cache_k, cache_v, cache_ki, k_new, v_new, ki_new, *, sk):
    bsz, p_len, kw = cache_k.shape
    t_new = k_new.shape[1]
    ncb = p_len // sk
    nkb = ncb + 1

    def cache_spec(width):
        return pl.BlockSpec((1, sk, width), lambda b, j: (b, jnp.minimum(j, ncb - 1), 0))

    def new_spec(width):
        return pl.BlockSpec((1, t_new, width), lambda b, j: (b, 0, 0))

    return pl.pallas_call(
        functools.partial(_keys_kernel, n_cache_blocks=ncb),
        grid=(bsz, nkb),
        in_specs=[cache_spec(kw), cache_spec(kw), cache_spec(IDX_DIM),
                  new_spec(kw), new_spec(kw), new_spec(IDX_DIM)],
        out_specs=(
            pl.BlockSpec((1, sk, N_KV_HEADS * K_AUG), lambda b, j: (b, j, 0)),
            pl.BlockSpec((1, sk, IDX_DIM), lambda b, j: (b, j, 0)),
            pl.BlockSpec((1, 1, N_KV_HEADS * V_ROWS, sk), lambda b, j: (b, j, 0, 0)),
        ),
        out_shape=(
            jax.ShapeDtypeStruct((bsz, nkb * sk, N_KV_HEADS * K_AUG), BF16),
            jax.ShapeDtypeStruct((bsz, nkb * sk, IDX_DIM), BF16),
            jax.ShapeDtypeStruct((bsz, nkb, N_KV_HEADS * V_ROWS, sk), BF16),
        ),
        compiler_params=_cparams(("arbitrary", "arbitrary")),
        name="keys",
    )(cache_k, cache_v, cache_ki, k_new, v_new, ki_new)


def _dsa_call(qg, qiT, wiT, k_aug, vT_blk, ki_b, kmax, *, tq, sk, s_real, q_pos0, n_q_real, topk):
    bsz, aw, tq_total = qiT.shape
    n_kblocks = vT_blk.shape[1]
    s_pad = k_aug.shape[1]
    nq = tq_total // tq
    grp = N_HEADS // N_KV_HEADS
    kern = functools.partial(_dsa_kernel, tq=tq, sk=sk, s_real=s_real, n_kblocks=n_kblocks,
                             q_pos0=q_pos0, n_q_real=n_q_real, topk=topk)
    return pl.pallas_call(
        kern,
        grid=(bsz, nq),
        in_specs=[
            pl.BlockSpec((1, 1, N_KV_HEADS * HEAD_DIM, grp * tq), lambda b, i: (b, i, 0, 0)),
            pl.BlockSpec((1, aw, tq), lambda b, i: (b, 0, i)),
            pl.BlockSpec((1, N_IDX_HEADS, tq), lambda b, i: (b, 0, i)),
            pl.BlockSpec((1, s_pad, N_KV_HEADS * K_AUG), lambda b, i: (b, 0, 0)),
            pl.BlockSpec((1, n_kblocks, N_KV_HEADS * V_ROWS, sk), lambda b, i: (b, 0, 0, 0)),
            pl.BlockSpec((1, s_pad, IDX_DIM), lambda b, i: (b, 0, 0)),
            pl.BlockSpec((1, 1, N_KV_HEADS * K_AUG), lambda b, i: (b, 0, 0)),
        ],
        out_specs=pl.BlockSpec((1, tq, aw), lambda b, i: (b, i, 0)),
        out_shape=jax.ShapeDtypeStruct((bsz, tq_total, aw), BF16),
        scratch_shapes=[
            pltpu.VMEM((s_pad, tq), F32),
            pltpu.VMEM((N_KV_HEADS, 1, grp * tq), F32),
            pltpu.VMEM((N_KV_HEADS, V_ROWS, grp * tq), F32),
            pltpu.VMEM((N_KV_HEADS, K_AUG, grp * tq), BF16),
        ],
        compiler_params=_cparams(("arbitrary", "arbitrary")),
        name="dsa",
    )(qg, qiT, wiT, k_aug, vT_blk, ki_b, kmax)


def _ffn_kernel(x_ref, ya_ref, yb_ref, wout_ref, nf_ref, wgu_ref, wo_ref, o_ref):
    w = ya_ref.shape[1]
    d_ff = wo_ref.shape[0]
    x1 = x_ref[...] + jnp.dot(ya_ref[...], wout_ref[:w, :], preferred_element_type=F32)
    x1 = x1 + jnp.dot(yb_ref[...], wout_ref[w:, :], preferred_element_type=F32)
    ms = jnp.mean(x1 * x1, axis=-1, keepdims=True)
    hf = (x1 * lax.rsqrt(ms + EPS) * nf_ref[...]).astype(BF16)
    gt = jnp.dot(hf, wgu_ref[:, :d_ff], preferred_element_type=F32)
    up = jnp.dot(hf, wgu_ref[:, d_ff:], preferred_element_type=F32)
    act = (gt * jax.nn.sigmoid(gt) * up).astype(BF16)
    o_ref[...] = x1 + jnp.dot(act, wo_ref[...], preferred_element_type=F32)


def _ffn_call(x2, ya2, yb2, wts, *, tm):
    n, d = x2.shape
    w = ya2.shape[1]
    aw = yb2.shape[1]
    single = pl.Buffered(1)

    def wspec(shape):
        nd = len(shape)
        return pl.BlockSpec(shape, lambda *_: (0,) * nd, pipeline_mode=single)

    names = ("w_out", "norm_ffn", "w_gu", "w_o")
    return pl.pallas_call(
        _ffn_kernel,
        grid=(n // tm,),
        in_specs=[
            pl.BlockSpec((tm, d), lambda r: (r, 0)),
            pl.BlockSpec((tm, w), lambda r: (r, 0)),
            pl.BlockSpec((tm, aw), lambda r: (r, 0)),
        ] + [wspec(wts[k].shape) for k in names],
        out_specs=pl.BlockSpec((tm, d), lambda r: (r, 0)),
        out_shape=jax.ShapeDtypeStruct((n, d), F32),
        compiler_params=_cparams(("arbitrary",)),
        name="ffn",
    )(x2, ya2, yb2, *[wts[k] for k in names])


def _block_diag(wb):
    n, a, b = wb.shape
    eye = jnp.eye(n, dtype=wb.dtype)
    return (eye[:, None, :, None] * wb[:, :, None, :]).reshape(n * a, n * b)


def _prep_weights(norm_mix, w_in, conv_w, conv_b, w_rg, b_rg, w_ig, b_ig, lru_lambda,
                  q_norm, k_norm, w_out, norm_ffn, w_ffn_in, w_ffn_out):
    d = w_in.shape[0]
    w = conv_b.shape[0]
    aw = N_HEADS * HEAD_DIM
    kw = N_KV_HEADS * HEAD_DIM
    iw = N_IDX_HEADS * IDX_DIM
    o = np.cumsum([0, w, w, aw, kw, kw, iw, IDX_DIM, N_IDX_HEADS])
    wb = w_in.astype(BF16)
    w_kw = jnp.pad(wb[:, o[6]:o[8]], ((0, 0), (0, LANES - (IDX_DIM + N_IDX_HEADS))))
    hd_id = np.arange(aw) // HEAD_DIM
    aug_id = np.arange(N_KV_HEADS * K_AUG)
    return {
        "norm_mix": norm_mix.reshape(1, d),
        "w_in": wb,
        "w_kw": w_kw,
        "conv_w": jnp.pad(conv_w, ((0, SUBLANES - CONV_W), (0, 0))),
        "conv_b": conv_b.reshape(1, w),
        "w_rg": _block_diag(w_rg).astype(BF16),
        "w_ig": _block_diag(w_ig).astype(BF16),
        "b_rg": b_rg.reshape(1, w),
        "b_ig": b_ig.reshape(1, w),
        "lam": lru_lambda.reshape(1, w),
        "q_norm": jnp.tile(q_norm, N_HEADS).reshape(1, aw),
        "k_norm": jnp.tile(k_norm, N_KV_HEADS).reshape(1, kw),
        "g": jnp.asarray(hd_id[:, None] == hd_id[None, :], BF16),
        "g_aug": jnp.asarray((aug_id[:, None] // K_AUG == aug_id[None, :] // K_AUG)
                             & (aug_id[:, None] % K_AUG < HEAD_DIM), BF16),
        "w_out": w_out.astype(BF16),
        "norm_ffn": norm_ffn.reshape(1, d),
        "w_gu": w_ffn_in.astype(BF16),
        "w_o": w_ffn_out.astype(BF16),
    }


def _rope_tables(pos):
    half = HEAD_DIM // 2
    inv = ROPE_THETA ** (-jnp.arange(half, dtype=F32) / half)
    ang = pos.astype(F32)[:, None] * inv[None, :]
    cos = jnp.cos(ang)
    sin = jnp.sin(ang)
    reps = LANES // HEAD_DIM
    return (jnp.tile(jnp.concatenate([cos, cos], axis=1), (1, reps)),
            jnp.tile(jnp.concatenate([-sin, sin], axis=1), (1, reps)))


def _query_operands(q, qi, wi, tq):
    bsz, t_len, _ = q.shape
    pad_q = ((0, 0), (0, tq - t_len), (0, 0))
    grp = N_HEADS // N_KV_HEADS
    qg = jnp.pad(q, pad_q).reshape(bsz, 1, tq, N_KV_HEADS, grp, HEAD_DIM)
    qg = qg.transpose(0, 1, 3, 5, 4, 2).reshape(bsz, 1, N_KV_HEADS * HEAD_DIM, grp * tq)
    return qg, jnp.pad(qi, pad_q).transpose(0, 2, 1), jnp.pad(wi, pad_q).transpose(0, 2, 1)


def _layer(x, pos0, conv_prev, h_prev, k_past, v_past, ki_past, wts):
    bsz, t_len, d = x.shape
    w = conv_prev.shape[2]
    kw = N_KV_HEADS * HEAD_DIM
    pos = pos0 + jnp.arange(t_len, dtype=jnp.int32)
    cos, sin = _rope_tables(pos)
    sk = DSA_KEY_BLOCK
    cprev = jnp.pad(conv_prev, ((0, 0), (SUBLANES - (CONV_W - 1), 0), (0, 0)))
    hprev = h_prev.reshape(bsz, 1, w)
    if k_past is None:
        assert t_len % sk == 0 and t_len % PROJ_ROWS == 0, (t_len, sk, PROJ_ROWS)
        tq = PROJ_ROWS
        outs = _proj_call(x, cos, sin, cprev, hprev, wts, tb=PROJ_ROWS, pos0=pos0, sk=sk)
        ya, k_new, v_new, ki_new, tail, hlast, qg, qiT, wiT, k_aug, ki_b, vT_blk = outs
        s_real = t_len
    else:
        p_len = k_past.shape[1]
        tq = LANES
        assert p_len % sk == 0 and t_len <= min(sk, tq), (p_len, t_len, sk)
        outs = _proj_call(x, cos, sin, cprev, hprev, wts, tb=t_len, pos0=pos0)
        ya, k_new, v_new, ki_new, tail, hlast, q, qi, kwi = outs
        qg, qiT, wiT = _query_operands(q, qi, kwi[..., IDX_DIM:IDX_DIM + N_IDX_HEADS], tq)
        k_aug, ki_b, vT_blk = _keys_call(k_past.reshape(bsz, p_len, kw), v_past.reshape(bsz, p_len, kw),
                                         ki_past, k_new, v_new, ki_new, sk=sk)
        s_real = p_len + t_len
    topk = min(TOPK_MAX, s_real // 4)
    kmax = _kmax_call(k_aug, wts["g_aug"])
    yb = _dsa_call(qg, qiT, wiT, k_aug, vT_blk, ki_b, kmax, tq=tq, sk=sk, s_real=s_real,
                   q_pos0=pos0, n_q_real=t_len, topk=topk)[:, :t_len]

    n = bsz * t_len
    tm = 512 if n % 512 == 0 else n
    y = _ffn_call(x.reshape(n, d), ya.reshape(n, w), yb.reshape(n, N_HEADS * HEAD_DIM), wts, tm=tm)
    return (y.reshape(bsz, t_len, d),
            k_new.reshape(bsz, t_len, N_KV_HEADS, HEAD_DIM),
            v_new.reshape(bsz, t_len, N_KV_HEADS, HEAD_DIM),
            ki_new,
            hlast[:, SUBLANES - 1],
            tail[:, SUBLANES - (CONV_W - 1):])


def kernel(x_prompt, x_sample, cache_k, cache_v, cache_kidx, state_h, state_conv, norm_mix, w_in, conv_w, conv_b, w_rg, b_rg, w_ig, b_ig, lru_lambda, q_norm, k_norm, w_out, norm_ffn, w_ffn_in, w_ffn_out):
    wts = _prep_weights(norm_mix, w_in, conv_w, conv_b, w_rg, b_rg, w_ig, b_ig, lru_lambda,
                        q_norm, k_norm, w_out, norm_ffn, w_ffn_in, w_ffn_out)
    bp = x_prompt.shape[0]
    w = conv_b.shape[0]
    p_len = cache_k.shape[1]
    conv0 = jnp.zeros((bp, CONV_W - 1, w), x_prompt.dtype)
    h0 = jnp.zeros((bp, w), x_prompt.dtype)
    yp, k_p, v_p, ki_p, h_p, conv_p = _layer(x_prompt, 0, conv0, h0, None, None, None, wts)
    ys, k_s, v_s, ki_s, h_s, conv_s = _layer(x_sample, p_len, state_conv, state_h,
                                             cache_k, cache_v, cache_kidx, wts)
    return (yp, ys, k_p, v_p, ki_p, h_p, conv_p, k_s, v_s, ki_s, h_s, conv_s)
```
